```python
import math
import jax
import jax.numpy as jnp
from jax import lax
import numpy as np

D_MODEL = 1024
BATCH = 4
SEQ = 4096
DEPTH = 4
DEC_BATCH = 32
DEC_SEQ = 8
PAST_LEN = 8192
PAGE_SIZE = 128

N_META = 16
H_A = 8
N_A = 64
W_A = H_A * N_A
R_W = 64
R_A = 64
R_G = 128
H_B = 4
DK_B = 64
DV_B = 128
QK_B = H_B * DK_B
W_B = H_B * DV_B
CHUNK_B = 128
ROPE_BASE = 10000.0
H_C = 4
DH_C = 64
DV_C = 2 * DH_C
W_C = H_C * DV_C
Q_BLOCK = 128
D_FF = 2816
CONV_W = 3
N_BRANCH = 3
A_COLS = 3 * W_A + R_W + R_A + R_G
B_COLS = 2 * QK_B + 2 * W_B
C_COLS = 3 * W_C
GATE_COLS = N_BRANCH * D_MODEL
N_IN = A_COLS + B_COLS + C_COLS + GATE_COLS
NORM_EPS = 1e-6
GN_EPS_A = 64e-5
SUBLN_EPS = 1e-5
NEG_INF = -1e30
F32 = jnp.float32

kernel_name = 'hybrid_rwkv7_retnet_diffattn_step'


def rmsnorm(x, g, eps):
    xf = x.astype(F32)
    y = xf * lax.rsqrt(jnp.mean(xf * xf, -1, keepdims=True) + eps)
    return (y * g.astype(F32)).astype(x.dtype)


def split_cols(t, sizes):
    return jnp.split(t, np.cumsum(sizes)[:-1].tolist(), axis=-1)


def rotary(x, pos):
    half = x.shape[-1] // 2
    inv = ROPE_BASE ** (-jnp.arange(half, dtype=F32) / half)
    ang = pos.astype(F32)[:, None] * inv[None, :]
    cos = jnp.cos(ang)[None, :, None, :]
    sin = jnp.sin(ang)[None, :, None, :]
    x1, x2 = x[..., :half], x[..., half:]
    return jnp.concatenate([x1 * cos - x2 * sin, x1 * sin + x2 * cos], axis=-1)


def rwkv7_time_mix(pa, shift_prev, s0, mu, w0, w2, a0, a2, g2, k_k, k_a, r_k, lnx_w, lnx_b):
    b, t, _ = pa.shape
    pa = pa.astype(F32)
    prev = jnp.concatenate([shift_prev[:, None].astype(F32), pa[:, :-1]], axis=1)
    xm = pa + (prev - pa) * mu
    r, k, v, wd, ad, gd = split_cols(xm, [W_A, W_A, W_A, R_W, R_A, R_G])
    w = -jax.nn.softplus(-(w0 + jnp.tanh(wd) @ w2)) - 0.5
    decay = jnp.exp(-jnp.exp(w.astype(F32)))
    a = jax.nn.sigmoid(a0 + ad @ a2)
    g = jax.nn.sigmoid(gd) @ g2
    heads = lambda z: z.reshape(b, t, H_A, N_A)
    kk = heads(k * k_k)
    kk = kk * lax.rsqrt(jnp.maximum(jnp.sum(kk * kk, -1, keepdims=True), 1e-24))
    k = k * (1.0 + (a - 1.0) * k_a)
    r, k, v, decay, a = heads(r), heads(k), heads(v), heads(decay), heads(a)

    def step(S, inp):
        r_t, k_t, v_t, d_t, kk_t, a_t = inp
        sa = jnp.einsum('bhvk,bhk->bhv', S, -kk_t)
        S = (S * d_t[:, :, None, :] + sa[..., None] * (kk_t * a_t)[:, :, None, :]
             + v_t[..., None] * k_t[:, :, None, :])
        return S, jnp.einsum('bhvk,bhk->bhv', S, r_t)

    xs = tuple(jnp.moveaxis(z, 1, 0) for z in (r, k, v, decay, kk, a))
    s_t, y = lax.scan(step, s0.astype(F32), xs)
    y = jnp.moveaxis(y, 0, 1)
    yc = y - jnp.mean(y, -1, keepdims=True)
    yn = yc * lax.rsqrt(jnp.mean(yc * yc, -1, keepdims=True) + GN_EPS_A)
    bonus = jnp.sum(r * k * r_k, -1, keepdims=True) * v
    out = (yn.reshape(b, t, W_A) * lnx_w + lnx_b + bonus.reshape(b, t, W_A)) * g
    return out, s_t, pa[:, -1]


def retention_log_decay():
    return jnp.log1p(-jnp.exp2(-5.0 - jnp.arange(H_B, dtype=F32)))


def retention_chunk(q, k, v, R):
    lg = retention_log_decay()
    L = q.shape[1]
    idx = jnp.arange(L, dtype=F32)
    diff = idx[:, None] - idx[None, :]
    dmat = jnp.where(diff >= 0, jnp.exp(jnp.maximum(diff, 0.0)[None] * lg[:, None, None]), 0.0)
    s = jnp.einsum('bihd,bjhd->bhij', q, k) * dmat
    o = jnp.einsum('bhij,bjhe->bihe', s, v)
    o = o + jnp.einsum('bihd,bhde->bihe', q, R.astype(F32)) * jnp.exp((idx[:, None] + 1.0) * lg[None, :])[None, :, :, None]
    R = (R.astype(F32) * jnp.exp(L * lg)[None, :, None, None]
         + jnp.einsum('bjhd,bjhe,jh->bhde', k, v, jnp.exp((L - 1.0 - idx)[:, None] * lg[None, :])))
    return o, R


def retention_prompt(q, k, v, R0):
    b, t = q.shape[:2]
    o_meta, R = retention_chunk(q[:, :N_META], k[:, :N_META], v[:, :N_META], R0)
    n_chunks = (t - N_META) // CHUNK_B

    def to_chunks(z):
        return jnp.moveaxis(z[:, N_META:].reshape(b, n_chunks, CHUNK_B, z.shape[2], z.shape[3]), 1, 0)

    def step(R, c):
        o, R = retention_chunk(c[0], c[1], c[2], R)
        return R, o

    R, o = lax.scan(step, R, (to_chunks(q), to_chunks(k), to_chunks(v)))
    o = jnp.moveaxis(o, 0, 1).reshape(b, t - N_META, H_B, DV_B)
    return jnp.concatenate([o_meta, o], axis=1), R


def diff_attn_block(q, k, v, q_pos, k_pos, lam):
    b, lq = q.shape[:2]
    lk = k.shape[1]
    q = q.astype(F32).reshape(b, lq, H_C, 2, DH_C) * (DH_C ** -0.5)
    k = k.astype(F32).reshape(b, lk, H_C, 2, DH_C)
    s = jnp.einsum('bqhmd,bkhmd->bhmqk', q, k)
    causal = k_pos[None, :] <= q_pos[:, None]
    p = jax.nn.softmax(jnp.where(causal, s, NEG_INF), axis=-1)
    attn = p[:, :, 0] - lam * p[:, :, 1]
    return jnp.einsum('bhqk,bkhe->bqhe', attn, v.astype(F32))


def diff_attn_prompt(q, k, v, lam):
    b, t = q.shape[:2]
    pos = jnp.arange(t)
    o_meta = diff_attn_block(q[:, :N_META], k[:, :N_META], v[:, :N_META], pos[:N_META], pos[:N_META], lam)
    nb = (t - N_META) // Q_BLOCK
    qb = jnp.moveaxis(q[:, N_META:].reshape(b, nb, Q_BLOCK, H_C, 2 * DH_C), 1, 0)
    pb = pos[N_META:].reshape(nb, Q_BLOCK)
    o = lax.map(lambda args: diff_attn_block(args[0], k, v, args[1], pos, lam), (qb, pb))
    o = jnp.moveaxis(o, 0, 1).reshape(b, t - N_META, H_C, DV_C)
    return jnp.concatenate([o_meta, o], axis=1)


def trunk(x, pos, wkv0, shift0, ret0, conv0, ret_core, attn_core, W):
    b, t, _ = x.shape
    dt = x.dtype
    outs = ([], [], [], [], [], [])
    for l in range(DEPTH):
        h = rmsnorm(x, W['norm1'][l], NORM_EPS)
        proj = h @ W['w_in'][l]
        pa, pb, pc, pg = split_cols(proj, [A_COLS, B_COLS, C_COLS, GATE_COLS])
        out_a, wkv_t, shift_t = rwkv7_time_mix(
            pa, shift0[l], wkv0[l], W['mu_a'][l], W['w0_a'][l], W['w2_a'][l], W['a0_a'][l],
            W['a2_a'][l], W['g2_a'][l], W['k_k'][l], W['k_a'][l], W['r_k'][l], W['lnx_w'][l], W['lnx_b'][l])
        qb, kb, vb, gb = split_cols(pb.astype(F32), [QK_B, QK_B, W_B, W_B])
        qb = rotary(qb.reshape(b, t, H_B, DK_B), pos)
        kb = rotary(kb.reshape(b, t, H_B, DK_B), pos) * (DK_B ** -0.5)
        ob, ret_t = ret_core(qb, kb, vb.reshape(b, t, H_B, DV_B), ret0[l])
        ob = ob * lax.rsqrt(jnp.mean(ob * ob, -1, keepdims=True) + NORM_EPS)
        out_b = ob.reshape(b, t, W_B) * jax.nn.silu(gb)
        qc, kc, vc = split_cols(pc, [W_C, W_C, W_C])
        qc = qc.reshape(b, t, H_C, 2 * DH_C)
        kc = kc.reshape(b, t, H_C, 2 * DH_C)
        vc = vc.reshape(b, t, H_C, DV_C)
        lam_init = 0.8 - 0.6 * math.exp(-0.3 * l)
        lam = (jnp.exp(jnp.sum(W['lam_q1'][l].astype(F32) * W['lam_k1'][l].astype(F32)))
               - jnp.exp(jnp.sum(W['lam_q2'][l].astype(F32) * W['lam_k2'][l].astype(F32))) + lam_init)
        oc = attn_core(l, qc, kc, vc, lam)
        oc = rmsnorm(oc, W['subln'][l], SUBLN_EPS) * (1.0 - lam_init)
        out_c = oc.reshape(b, t, W_C)
        g_a, g_b, g_c = split_cols(jax.nn.sigmoid(pg.astype(F32)), [D_MODEL] * N_BRANCH)
        merged = (g_a * (out_a @ W['w_pa'][l]) + g_b * (out_b @ W['w_pb'][l])
                  + g_c * (out_c @ W['w_pc'][l]))
        x = x + (merged @ W['w_o'][l]).astype(dt)
        h2 = rmsnorm(x, W['norm2'][l], NORM_EPS)
        u, g = split_cols(h2 @ W['w_ffn_in'][l], [D_FF, D_FF])
        ext = jnp.concatenate([conv0[l].astype(u.dtype), u], axis=1)
        cw = W['conv_w'][l]
        uc = W['conv_b'][l] + ext[:, 0:t] * cw[0]
        for j in range(1, CONV_W):
            uc = uc + ext[:, j:j + t] * cw[j]
        x = x + ((jax.nn.gelu(uc) * g) @ W['w_ffn_out'][l]).astype(dt)
        for lst, val in zip(outs, (kc, vc, wkv_t, shift_t, ret_t, ext[:, t:])):
            lst.append(val)
    y = rmsnorm(x, W['norm_f'], NORM_EPS)
    return y, [jnp.stack(lst, 0) for lst in outs]


def setup_inputs(seed: int = 0) -> dict:
    key = jax.random.key(seed)
    ks = iter(jax.random.split(key, 48))

    def nrm(shape, scale):
        return jax.random.normal(next(ks), shape, F32) * scale

    def gain(shape):
        return 1.0 + nrm(shape, 0.02)

    n_pages = PAST_LEN // PAGE_SIZE
    n_pool = (DEC_BATCH * n_pages * 5) // 4
    page_table = jax.random.permutation(next(ks), n_pool)[:DEC_BATCH * n_pages].reshape(DEC_BATCH, n_pages).astype(jnp.int32)
    return {
        'x_prompt': nrm((BATCH, SEQ, D_MODEL), 1.0),
        'x_sample': nrm((DEC_BATCH, DEC_SEQ, D_MODEL), 1.0),
        'cache_k': nrm((DEPTH, n_pool, PAGE_SIZE, H_C, 2 * DH_C), 1.0),
        'cache_v': nrm((DEPTH, n_pool, PAGE_SIZE, H_C, DV_C), 1.0),
        'page_table': page_table,
        'state_wkv': nrm((DEPTH, DEC_BATCH, H_A, N_A, N_A), 0.3),
        'state_shift': nrm((DEPTH, DEC_BATCH, A_COLS), 1.0),
        'state_ret': nrm((DEPTH, DEC_BATCH, H_B, DK_B, DV_B), 1.0),
        'state_conv': nrm((DEPTH, DEC_BATCH, CONV_W - 1, D_FF), 1.0),
        'meta': nrm((N_META, D_MODEL), 1.0),
        'norm1': gain((DEPTH, D_MODEL)),
        'w_in': nrm((DEPTH, D_MODEL, N_IN), D_MODEL ** -0.5),
        'mu_a': jax.random.uniform(next(ks), (DEPTH, A_COLS), F32),
        'w0_a': jax.random.uniform(next(ks), (DEPTH, W_A), F32, -4.0, -0.5),
        'w2_a': nrm((DEPTH, R_W, W_A), 0.5 * R_W ** -0.5),
        'a0_a': nrm((DEPTH, W_A), 0.1),
        'a2_a': nrm((DEPTH, R_A, W_A), R_A ** -0.5),
        'g2_a': nrm((DEPTH, R_G, W_A), R_G ** -0.5),
        'k_k': 0.85 + nrm((DEPTH, W_A), 0.02),
        'k_a': gain((DEPTH, W_A)),
        'r_k': nrm((DEPTH, H_A, N_A), 0.1),
        'lnx_w': gain((DEPTH, W_A)),
        'lnx_b': nrm((DEPTH, W_A), 0.01),
        'lam_q1': nrm((DEPTH, DH_C), 0.1),
        'lam_k1': nrm((DEPTH, DH_C), 0.1),
        'lam_q2': nrm((DEPTH, DH_C), 0.1),
        'lam_k2': nrm((DEPTH, DH_C), 0.1),
        'subln': gain((DEPTH, DV_C)),
        'w_pa': nrm((DEPTH, W_A, D_MODEL), W_A ** -0.5),
        'w_pb': nrm((DEPTH, W_B, D_MODEL), W_B ** -0.5),
        'w_pc': nrm((DEPTH, W_C, D_MODEL), W_C ** -0.5),
        'w_o': nrm((DEPTH, D_MODEL, D_MODEL), D_MODEL ** -0.5),
        'norm2': gain((DEPTH, D_MODEL)),
        'w_ffn_in': nrm((DEPTH, D_MODEL, 2 * D_FF), D_MODEL ** -0.5),
        'conv_w': nrm((DEPTH, CONV_W, D_FF), CONV_W ** -0.5),
        'conv_b': nrm((DEPTH, D_FF), 0.01),
        'w_ffn_out': nrm((DEPTH, D_FF, D_MODEL), D_FF ** -0.5),
        'norm_f': gain((D_MODEL,)),
    }


def reference(x_prompt, x_sample, cache_k, cache_v, page_table, state_wkv, state_shift, state_ret,
              state_conv, meta, norm1, w_in, mu_a, w0_a, w2_a, a0_a, a2_a, g2_a, k_k, k_a, r_k,
              lnx_w, lnx_b, lam_q1, lam_k1, lam_q2, lam_k2, subln, w_pa, w_pb, w_pc, w_o, norm2,
              w_ffn_in, conv_w, conv_b, w_ffn_out, norm_f):
    W = dict(norm1=norm1, w_in=w_in, mu_a=mu_a, w0_a=w0_a, w2_a=w2_a, a0_a=a0_a, a2_a=a2_a,
             g2_a=g2_a, k_k=k_k, k_a=k_a, r_k=r_k, lnx_w=lnx_w, lnx_b=lnx_b, lam_q1=lam_q1,
             lam_k1=lam_k1, lam_q2=lam_q2, lam_k2=lam_k2, subln=subln, w_pa=w_pa, w_pb=w_pb,
             w_pc=w_pc, w_o=w_o, norm2=norm2, w_ffn_in=w_ffn_in, conv_w=conv_w, conv_b=conv_b,
             w_ffn_out=w_ffn_out, norm_f=norm_f)

    b = x_prompt.shape[0]
    xp = jnp.concatenate([jnp.broadcast_to(meta[None].astype(x_prompt.dtype), (b, N_META, D_MODEL)), x_prompt], axis=1)
    tp = xp.shape[1]
    yp, st_p = trunk(
        xp, jnp.arange(tp),
        jnp.zeros((DEPTH, b, H_A, N_A, N_A), F32), jnp.zeros((DEPTH, b, A_COLS), F32),
        jnp.zeros((DEPTH, b, H_B, DK_B, DV_B), F32), jnp.zeros((DEPTH, b, CONV_W - 1, D_FF), F32),
        retention_prompt, lambda l, q, k, v, lam: diff_attn_prompt(q, k, v, lam), W)
    y_prompt = yp[:, N_META:]
    k_p, v_p, wkv_p, sh_p, ret_p, conv_p = st_p

    bd, ds = x_sample.shape[:2]
    n_pages = page_table.shape[1]
    past = n_pages * PAGE_SIZE

    def sample_attn(l, q, k, v, lam):
        pk = cache_k[l][page_table].reshape(bd, past, H_C, 2 * DH_C)
        pv = cache_v[l][page_table].reshape(bd, past, H_C, DV_C)
        kk = jnp.concatenate([pk.astype(F32), k.astype(F32)], axis=1)
        vv = jnp.concatenate([pv.astype(F32), v.astype(F32)], axis=1)
        return diff_attn_block(q, kk, vv, past + jnp.arange(ds), jnp.arange(past + ds), lam)

    y_sample, st_s = trunk(x_sample, past + jnp.arange(ds), state_wkv, state_shift, state_ret,
                           state_conv, retention_chunk, sample_attn, W)
    k_s, v_s, wkv_s, sh_s, ret_s, conv_s = st_s
    return (y_prompt, y_sample, k_p, v_p, wkv_p, sh_p, ret_p, conv_p, k_s, v_s, wkv_s, sh_s, ret_s, conv_s)
```

```python
import math

import jax
import jax.numpy as jnp
from jax import lax
from jax.experimental import pallas as pl
from jax.experimental.pallas import tpu as pltpu

F32 = jnp.float32
BF16 = jnp.bfloat16

D_MODEL = 1024
DEPTH = 4
N_META = 16
H_A, N_A = 8, 64
W_A = H_A * N_A
R_W, R_A, R_G = 64, 64, 128
H_B, DK_B, DV_B = 4, 64, 128
QK_B, W_B = H_B * DK_B, H_B * DV_B
ROPE_BASE = 10000.0
H_C, DH_C = 4, 64
DV_C = 2 * DH_C
W_C = H_C * DV_C
D_FF = 2816
CONV_W = 3
A_COLS = 3 * W_A + R_W + R_A + R_G
B_COLS = 2 * QK_B + 2 * W_B
C_COLS = 3 * W_C
NORM_EPS = 1e-6
GN_EPS_A = 64e-5
SUBLN_EPS = 1e-5
NEG_INF = -1e30
PAGE_SIZE = 128

LANES = 128
VMEM_LIMIT = 52 * 1024 * 1024
ROW_TILE = 512
RWKV_SEQS = 4
RWKV_TILE = 256
RET_CHUNK = 128
ATTN_BLOCK = 512
PAGES_PER_STEP = 8

HIGHEST = lax.Precision.HIGHEST


def _cparams(sem):
    return pltpu.CompilerParams(dimension_semantics=sem, vmem_limit_bytes=VMEM_LIMIT)


def _rms(x, gain, eps):
    return x * lax.rsqrt(jnp.mean(x * x, axis=-1, keepdims=True) + eps) * gain


def _bdot(a, b):
    return jnp.dot(a.astype(BF16), b.astype(BF16), preferred_element_type=F32)


def _bdot_nt(a, b):
    return lax.dot_general(a.astype(BF16), b.astype(BF16), (((1,), (1,)), ((), ())),
                           preferred_element_type=F32)


def _bdot_tn(a, b):
    return lax.dot_general(a.astype(BF16), b.astype(BF16), (((0,), (0,)), ((), ())),
                           preferred_element_type=F32)


def _fdot(a, b):
    return jnp.dot(a, b, precision=HIGHEST, preferred_element_type=F32)


def _full(shape):
    return pl.BlockSpec(shape, lambda *_: (0,) * len(shape))


def _layer(shape, layer):
    return pl.BlockSpec((None,) + shape, lambda *_: (layer,) + (0,) * len(shape))


def _in_proj_kernel(x_ref, g_ref, wa_ref, wb_ref, wc_ref, pa_ref, pb_ref, q_ref, k_ref, v_ref):
    h = _rms(x_ref[...], g_ref[...], NORM_EPS).astype(BF16)
    pa_ref[...] = jnp.dot(h, wa_ref[...], preferred_element_type=F32)
    pb_ref[...] = jnp.dot(h, wb_ref[...], preferred_element_type=F32)
    q_ref[...] = jnp.dot(h, wc_ref[:, 0:W_C], preferred_element_type=F32)
    k_ref[...] = jnp.dot(h, wc_ref[:, W_C:2 * W_C], preferred_element_type=F32)
    v_ref[...] = jnp.dot(h, wc_ref[:, 2 * W_C:3 * W_C], preferred_element_type=F32)


def in_proj(x, gain, wa, wb, wc, layer):
    n = x.shape[0]
    tm = min(ROW_TILE, n)
    row = lambda c: pl.BlockSpec((tm, c), lambda i: (i, 0))
    return pl.pallas_call(
        _in_proj_kernel,
        grid=(n // tm,),
        in_specs=[row(D_MODEL), _layer((1, D_MODEL), layer), _layer((D_MODEL, A_COLS), layer),
                  _layer((D_MODEL, B_COLS), layer), _layer((D_MODEL, C_COLS), layer)],
        out_specs=[row(A_COLS), row(B_COLS), row(W_C), row(W_C), row(W_C)],
        out_shape=[jax.ShapeDtypeStruct((n, c), F32) for c in (A_COLS, B_COLS, W_C, W_C, W_C)],
        compiler_params=_cparams(("parallel",)),
        name="in_proj",
    )(x, gain, wa, wb, wc)


def _head_ones():
    r = lax.broadcasted_iota(jnp.int32, (W_A, W_A), 0) // N_A
    c = lax.broadcasted_iota(jnp.int32, (W_A, W_A), 1) // N_A
    return (r == c).astype(F32)


def _rwkv_kernel(pa_ref, shift0_ref, s0_ref, mu_ref, w0_ref, w2_ref, a0_ref, a2_ref, g2_ref,
                 kk_ref, ka_ref, rk_ref, lnw_ref, lnb_ref,
                 out_ref, sT_ref, shiftT_ref,
                 s_scr, carry_scr, r_scr, k_scr, v_scr, d_scr, al_scr, be_scr, y_scr, g_scr, bonus_scr):
    nb, lt = pa_ref.shape[0], pa_ref.shape[1]
    npair = H_A // 2
    ti = pl.program_id(1)

    @pl.when(ti == 0)
    def _():
        s_scr[...] = s0_ref[...].reshape(nb * npair, N_A, LANES)
        carry_scr[...] = shift0_ref[...]

    ones = _head_ones()
    row = lax.broadcasted_iota(jnp.int32, (lt, A_COLS), 0)
    for b in range(nb):
        x = pa_ref[b]
        prev = jnp.where(row == 0, carry_scr[b], pltpu.roll(x, 1, axis=0))
        carry_scr[b] = x[lt - 1:lt, :]
        xm = x + (prev - x) * mu_ref[...]
        r = xm[:, 0:W_A]
        k = xm[:, W_A:2 * W_A]
        v = xm[:, 2 * W_A:3 * W_A]
        lr = xm[:, 3 * W_A:3 * W_A + R_W + R_A]
        gd = xm[:, 3 * W_A + R_W + R_A:A_COLS]
        w = -jax.nn.softplus(-(w0_ref[...] + _fdot(jnp.tanh(lr), w2_ref[...]))) - 0.5
        a = jax.nn.sigmoid(a0_ref[...] + _fdot(lr, a2_ref[...]))
        kk = k * kk_ref[...]
        kk = kk * lax.rsqrt(jnp.maximum(_fdot(kk * kk, ones), 1e-24))
        k = k * (1.0 + (a - 1.0) * ka_ref[...])
        g_scr[b] = _fdot(jax.nn.sigmoid(gd), g2_ref[...])
        bonus_scr[b] = _fdot(r * k * rk_ref[...], ones) * v
        for z, scr in ((r, r_scr), (k, k_scr), (v, v_scr), (jnp.exp(-jnp.exp(w)), d_scr),
                       (-kk, al_scr), (kk * a, be_scr)):
            for p in range(npair):
                scr[b * npair + p] = z[:, p * LANES:(p + 1) * LANES]

    lane = lax.broadcasted_iota(jnp.int32, (N_A, LANES), 1)
    sub = lax.broadcasted_iota(jnp.int32, (N_A, LANES), 0)
    lo = lane < N_A
    diag = (lane % N_A) == sub

    def seg_sum(z):
        s_lo = jnp.sum(jnp.where(lo, z, 0.0), axis=-1, keepdims=True)
        s_hi = jnp.sum(jnp.where(lo, 0.0, z), axis=-1, keepdims=True)
        return jnp.where(lo, s_lo, s_hi)

    def step(t, carry):
        for i in range(nb * npair):
            ld = lambda ref: ref[i, pl.ds(t, 1), :]
            s = s_scr[i]
            sa = seg_sum(s * ld(al_scr))
            vcol = seg_sum(jnp.where(diag, ld(v_scr), 0.0))
            s = s * ld(d_scr) + sa * ld(be_scr) + vcol * ld(k_scr)
            s_scr[i] = s
            yb = seg_sum(s * ld(r_scr))
            y_scr[i, pl.ds(t, 1), :] = jnp.sum(jnp.where(diag, yb, 0.0), axis=0, keepdims=True)
        return carry

    lax.fori_loop(0, lt, step, 0)

    for b in range(nb):
        y = jnp.concatenate([y_scr[b * npair + p] for p in range(npair)], axis=1)
        yc = y - _fdot(y, ones) * (1.0 / N_A)
        yn = yc * lax.rsqrt(_fdot(yc * yc, ones) * (1.0 / N_A) + GN_EPS_A)
        out_ref[b] = (yn * lnw_ref[...] + lnb_ref[...] + bonus_scr[b]) * g_scr[b]

    @pl.when(ti == pl.num_programs(1) - 1)
    def _():
        sT_ref[...] = s_scr[...].reshape(nb, npair, N_A, LANES)
        shiftT_ref[...] = carry_scr[...]


def _pack_wkv(s):
    n = s.shape[0]
    return s.reshape(n, H_A // 2, 2, N_A, N_A).transpose(0, 1, 3, 2, 4).reshape(n, H_A // 2, N_A, LANES)


def _unpack_wkv(s):
    n = s.shape[0]
    return s.reshape(n, H_A // 2, N_A, 2, N_A).transpose(0, 1, 3, 2, 4).reshape(n, H_A, N_A, N_A)


def rwkv(pa, shift0, wkv0, P, layer):
    nseq, L, _ = pa.shape
    nb = RWKV_SEQS
    lt = min(RWKV_TILE, L)
    npair = H_A // 2
    seq3 = lambda c: pl.BlockSpec((nb, lt, c), lambda s, t: (s, t, 0))
    vec = lambda name: _layer(P[name].shape[1:], layer)
    slab = lambda: pltpu.VMEM((nb * npair, lt, LANES), F32)
    tile = lambda: pltpu.VMEM((nb, lt, W_A), F32)
    out, sT, shiftT = pl.pallas_call(
        _rwkv_kernel,
        grid=(nseq // nb, L // lt),
        in_specs=[seq3(A_COLS),
                  pl.BlockSpec((nb, 1, A_COLS), lambda s, t: (s, 0, 0)),
                  pl.BlockSpec((nb, npair, N_A, LANES), lambda s, t: (s, 0, 0, 0)),
                  vec('mu_a'), vec('w0_a'), vec('w2_a'), vec('a0_a'), vec('a2_a'), vec('g2_a'),
                  vec('k_k'), vec('k_a'), vec('r_k'), vec('lnx_w'), vec('lnx_b')],
        out_specs=[seq3(W_A),
                   pl.BlockSpec((nb, npair, N_A, LANES), lambda s, t: (s, 0, 0, 0)),
                   pl.BlockSpec((nb, 1, A_COLS), lambda s, t: (s, 0, 0))],
        out_shape=[jax.ShapeDtypeStruct((nseq, L, W_A), F32),
                   jax.ShapeDtypeStruct((nseq, npair, N_A, LANES), F32),
                   jax.ShapeDtypeStruct((nseq, 1, A_COLS), F32)],
        scratch_shapes=[pltpu.VMEM((nb * npair, N_A, LANES), F32), pltpu.VMEM((nb, 1, A_COLS), F32)]
                       + [slab() for _ in range(7)] + [tile(), tile()],
        compiler_params=_cparams(("parallel", "arbitrary")),
        name="rwkv",
    )(pa, shift0[:, None, :], _pack_wkv(wkv0), P['mu_a'], P['w0_a'], P['w2_a'], P['a0_a'],
      P['a2_a'], P['g2_a'], P['k_k'], P['k_a'], P['r_k'], P['lnx_w'], P['lnx_b'])
    return out, _unpack_wkv(sT), shiftT[:, 0, :]


def _swap_halves(x):
    lane = lax.broadcasted_iota(jnp.int32, x.shape, 1)
    first = (lane % DK_B) < (DK_B // 2)
    return jnp.where(first, pltpu.roll(x, QK_B - DK_B // 2, axis=1), pltpu.roll(x, DK_B // 2, axis=1))


def _retention_kernel(pb_ref, r0_ref, cos_ref, sin_ref, dmat_ref, rowdec_ref, keydec_ref, chdec_ref,
                      out_ref, rT_ref, r_scr):
    ci = pl.program_id(1)

    @pl.when(ci == 0)
    def _():
        r_scr[...] = r0_ref[...]

    x = pb_ref[...]
    q = x[:, 0:QK_B]
    k = x[:, QK_B:2 * QK_B]
    q = q * cos_ref[...] + _swap_halves(q) * sin_ref[...]
    k = (k * cos_ref[...] + _swap_halves(k) * sin_ref[...]) * (DK_B ** -0.5)
    for h in range(H_B):
        qh = q[:, h * DK_B:(h + 1) * DK_B]
        kh = k[:, h * DK_B:(h + 1) * DK_B]
        vh = x[:, 2 * QK_B + h * DV_B:2 * QK_B + (h + 1) * DV_B]
        gh = x[:, 2 * QK_B + W_B + h * DV_B:2 * QK_B + W_B + (h + 1) * DV_B]
        rs = r_scr[h]
        s = _bdot_nt(qh, kh) * dmat_ref[h]
        o = _bdot(s, vh) + _bdot(qh, rs) * rowdec_ref[h]
        r_scr[h] = rs * chdec_ref[h] + _bdot_tn(kh * keydec_ref[h], vh)
        o = o * lax.rsqrt(jnp.mean(o * o, axis=-1, keepdims=True) + NORM_EPS)
        out_ref[:, h * DV_B:(h + 1) * DV_B] = o * jax.nn.silu(gh)

    @pl.when(ci == pl.num_programs(1) - 1)
    def _():
        rT_ref[...] = r_scr[...]


def _retention_tables(L, lc, pos0):
    lg = jnp.log1p(-jnp.exp2(-5.0 - jnp.arange(H_B, dtype=F32)))
    idx = jnp.arange(lc, dtype=F32)
    diff = idx[:, None] - idx[None, :]
    dmat = jnp.where(diff >= 0, jnp.exp(jnp.maximum(diff, 0.0)[None] * lg[:, None, None]), 0.0)
    rowdec = jnp.exp((idx[None, :] + 1.0) * lg[:, None])[..., None]
    keydec = jnp.exp((lc - 1.0 - idx)[None, :] * lg[:, None])[..., None]
    chdec = jnp.exp(lc * lg)[:, None, None]
    half = DK_B // 2
    inv = ROPE_BASE ** (-jnp.arange(half, dtype=F32) / half)
    ang = (pos0 + jnp.arange(L)).astype(F32)[:, None] * inv[None, :]
    cos, sin = jnp.cos(ang), jnp.sin(ang)
    cos_t = jnp.tile(jnp.concatenate([cos, cos], axis=1), (1, H_B))
    sin_t = jnp.tile(jnp.concatenate([-sin, sin], axis=1), (1, H_B))
    return cos_t, sin_t, dmat, rowdec, keydec, chdec


def retention(pb, ret0, pos0):
    nseq, L, _ = pb.shape
    lc = min(RET_CHUNK, L)
    cos_t, sin_t, dmat, rowdec, keydec, chdec = _retention_tables(L, lc, pos0)
    state = pl.BlockSpec((None, H_B, DK_B, DV_B), lambda s, c: (s, 0, 0, 0))
    tab = pl.BlockSpec((lc, QK_B), lambda s, c: (c, 0))
    out, rT = pl.pallas_call(
        _retention_kernel,
        grid=(nseq, L // lc),
        in_specs=[pl.BlockSpec((None, lc, B_COLS), lambda s, c: (s, c, 0)), state, tab, tab,
                  _full((H_B, lc, lc)), _full((H_B, lc, 1)), _full((H_B, lc, 1)), _full((H_B, 1, 1))],
        out_specs=[pl.BlockSpec((None, lc, W_B), lambda s, c: (s, c, 0)), state],
        out_shape=[jax.ShapeDtypeStruct((nseq, L, W_B), F32),
                   jax.ShapeDtypeStruct((nseq, H_B, DK_B, DV_B), F32)],
        scratch_shapes=[pltpu.VMEM((H_B, DK_B, DV_B), F32)],
        compiler_params=_cparams(("parallel", "arbitrary")),
        name="retention",
    )(pb, ret0, cos_t, sin_t, dmat, rowdec, keydec, chdec)
    return out, rT


def _lam(lq1, lk1, lq2, lk2, lam_init):
    return (jnp.exp(jnp.sum(lq1 * lk1, axis=-1, keepdims=True))
            - jnp.exp(jnp.sum(lq2 * lk2, axis=-1, keepdims=True)) + lam_init)


def _softmax_update(s, v, m_ref, l_ref, acc_ref):
    m_old = m_ref[...]
    m_new = jnp.maximum(m_old, jnp.max(s, axis=-1, keepdims=True))
    alpha = jnp.exp(m_old - m_new)
    p = jnp.exp(s - m_new)
    l_ref[...] = alpha * l_ref[...] + jnp.sum(p, axis=-1, keepdims=True)
    acc_ref[...] = alpha * acc_ref[...] + _bdot(p, v)
    m_ref[...] = m_new


def _attn_prompt_kernel(lam_init, has_prefix, q_ref, k_ref, v_ref, *rest):
    if has_prefix:
        kp_ref, vp_ref, *rest = rest
    (lq1, lk1, lq2, lk2, subln_ref, out_ref, m1, l1, a1, m2, l2, a2) = rest
    qi, kj = pl.program_id(2), pl.program_id(3)
    stats = ((m1, l1, a1), (m2, l2, a2))
    q = q_ref[...] * (DH_C ** -0.5)

    def process(k, v, mask):
        for m, (m_ref, l_ref, acc_ref) in enumerate(stats):
            s = _bdot_nt(q[:, m * DH_C:(m + 1) * DH_C], k[:, m * DH_C:(m + 1) * DH_C])
            if mask:
                r = lax.broadcasted_iota(jnp.int32, s.shape, 0)
                c = lax.broadcasted_iota(jnp.int32, s.shape, 1)
                s = jnp.where(c <= r, s, NEG_INF)
            _softmax_update(s, v, m_ref, l_ref, acc_ref)

    @pl.when(kj == 0)
    def _():
        for m_ref, l_ref, acc_ref in stats:
            m_ref[...] = jnp.full(m_ref.shape, NEG_INF, F32)
            l_ref[...] = jnp.zeros(l_ref.shape, F32)
            acc_ref[...] = jnp.zeros(acc_ref.shape, F32)
        if has_prefix:
            process(kp_ref[...], vp_ref[...], False)

    @pl.when(kj < qi)
    def _():
        process(k_ref[...], v_ref[...], False)

    @pl.when(kj == qi)
    def _():
        process(k_ref[...], v_ref[...], True)
        lam = _lam(lq1[...], lk1[...], lq2[...], lk2[...], lam_init)
        o = a1[...] / l1[...] - lam * (a2[...] / l2[...])
        out_ref[...] = _rms(o, subln_ref[...], SUBLN_EPS) * (1.0 - lam_init)


def attn_prompt(q, k, v, prefix, P, layer, lam_init):
    nseq, L, _ = q.shape
    blk = min(ATTN_BLOCK, L)
    nblk = L // blk
    qspec = pl.BlockSpec((None, blk, DV_C), lambda b, h, i, j: (b, i, h))
    kspec = pl.BlockSpec((None, blk, DV_C), lambda b, h, i, j: (b, jnp.minimum(i, j), h))
    args, specs = [q, k, v], [qspec, kspec, kspec]
    if prefix is not None:
        npre = prefix[0].shape[1]
        pspec = pl.BlockSpec((None, npre, DV_C), lambda b, h, i, j: (b, 0, h))
        args += list(prefix)
        specs += [pspec, pspec]
    for name in ('lam_q1', 'lam_k1', 'lam_q2', 'lam_k2'):
        args.append(P[name])
        specs.append(_layer((1, DH_C), layer))
    args.append(P['subln'])
    specs.append(_layer((1, DV_C), layer))
    stat = lambda: pltpu.VMEM((blk, 1), F32)
    acc = lambda: pltpu.VMEM((blk, DV_C), F32)
    return pl.pallas_call(
        lambda *refs: _attn_prompt_kernel(lam_init, prefix is not None, *refs),
        grid=(nseq, H_C, nblk, nblk),
        in_specs=specs,
        out_specs=qspec,
        out_shape=jax.ShapeDtypeStruct((nseq, L, W_C), F32),
        scratch_shapes=[stat(), stat(), acc(), stat(), stat(), acc()],
        compiler_params=_cparams(("parallel", "parallel", "parallel", "arbitrary")),
        name="attn_prompt",
    )(*args)


def _attn_sample_kernel(lam_init, npp, pt_ref, q_ref, kn_ref, vn_ref, lq1, lk1, lq2, lk2, subln_ref,
                        *rest):
    k_refs, v_refs = rest[:npp], rest[npp:2 * npp]
    out_ref, qb_scr, m_scr, l_scr, acc_scr = rest[2 * npp:]
    j = pl.program_id(1)
    nq = q_ref.shape[0]
    rows = H_C * 2 * nq

    @pl.when(j == 0)
    def _():
        q = q_ref[...] * (DH_C ** -0.5)
        qrep = jnp.broadcast_to(q[None], (H_C * 2, nq, W_C)).reshape(rows, W_C)
        r = lax.broadcasted_iota(jnp.int32, (rows, W_C), 0)
        c = lax.broadcasted_iota(jnp.int32, (rows, W_C), 1)
        qb_scr[...] = jnp.where(r // nq == c // DH_C, qrep, 0.0).astype(BF16)
        m_scr[...] = jnp.full(m_scr.shape, NEG_INF, F32)
        l_scr[...] = jnp.zeros(l_scr.shape, F32)
        acc_scr[...] = jnp.zeros(acc_scr.shape, F32)

    def update(s, pv):
        m_old = m_scr[...]
        m_new = jnp.maximum(m_old, jnp.max(s, axis=-1, keepdims=True))
        alpha = jnp.exp(m_old - m_new)
        p = jnp.exp(s - m_new)
        l_scr[...] = alpha * l_scr[...] + jnp.sum(p, axis=-1, keepdims=True)
        acc_scr[...] = alpha * acc_scr[...] + pv(p)
        m_scr[...] = m_new

    qb = qb_scr[...]
    s = jnp.concatenate([_bdot_nt(qb, k_refs[i][...]) for i in range(npp)], axis=1)

    def pv_pages(p):
        tot = _bdot(p[:, 0:PAGE_SIZE], v_refs[0][...])
        for i in range(1, npp):
            tot = tot + _bdot(p[:, i * PAGE_SIZE:(i + 1) * PAGE_SIZE], v_refs[i][...])
        return tot

    update(s, pv_pages)

    @pl.when(j == pl.num_programs(1) - 1)
    def _():
        sn = _bdot_nt(qb, kn_ref[...])
        r = lax.broadcasted_iota(jnp.int32, sn.shape, 0)
        c = lax.broadcasted_iota(jnp.int32, sn.shape, 1)
        sn = jnp.where(c <= r % nq, sn, NEG_INF)
        update(sn, lambda p: _bdot(p, vn_ref[...]))
        lam = _lam(lq1[...], lk1[...], lq2[...], lk2[...], lam_init)
        o_all = acc_scr[...] / l_scr[...]
        for h in range(H_C):
            cols = slice(h * DV_C, (h + 1) * DV_C)
            o = o_all[2 * h * nq:(2 * h + 1) * nq, cols] - lam * o_all[(2 * h + 1) * nq:(2 * h + 2) * nq, cols]
            out_ref[:, cols] = _rms(o, subln_ref[...], SUBLN_EPS) * (1.0 - lam_init)


def attn_sample(q, k, v, cache_k, cache_v, page_table, P, layer, lam_init):
    nseq, nq, _ = q.shape
    n_pages = page_table.shape[1]
    npp = PAGES_PER_STEP
    n_pool = cache_k.shape[1]
    ck = cache_k.reshape(DEPTH, n_pool, PAGE_SIZE, W_C)
    cv = cache_v.reshape(DEPTH, n_pool, PAGE_SIZE, W_C)
    seq = pl.BlockSpec((None, nq, W_C), lambda b, j, pt: (b, 0, 0))

    def page(i):
        return pl.BlockSpec((None, None, PAGE_SIZE, W_C),
                            lambda b, j, pt: (layer, pt[b * n_pages + j * npp + i], 0, 0))

    lspec = lambda c: pl.BlockSpec((None, 1, c), lambda b, j, pt: (layer, 0, 0))
    rows = H_C * 2 * nq
    grid_spec = pltpu.PrefetchScalarGridSpec(
        num_scalar_prefetch=1,
        grid=(nseq, n_pages // npp),
        in_specs=[seq, seq, seq, lspec(DH_C), lspec(DH_C), lspec(DH_C), lspec(DH_C), lspec(DV_C)]
                 + [page(i) for i in range(npp)] + [page(i) for i in range(npp)],
        out_specs=seq,
        scratch_shapes=[pltpu.VMEM((rows, W_C), BF16), pltpu.VMEM((rows, 1), F32),
                        pltpu.VMEM((rows, 1), F32), pltpu.VMEM((rows, W_C), F32)],
    )
    return pl.pallas_call(
        lambda *refs: _attn_sample_kernel(lam_init, npp, *refs),
        grid_spec=grid_spec,
        out_shape=jax.ShapeDtypeStruct((nseq, nq, W_C), F32),
        compiler_params=_cparams(("parallel", "arbitrary")),
        name="attn_sample",
    )(page_table.reshape(-1), q, k, v, P['lam_q1'], P['lam_k1'], P['lam_q2'], P['lam_k2'], P['subln'],
      *([ck] * npp), *([cv] * npp))


def _merge_kernel(x_ref, oa_ref, ob_ref, oc_ref, g_ref, wg_ref, wpa_ref, wpb_ref, wpc_ref, wo_ref,
                  out_ref):
    x = x_ref[...]
    h = _rms(x, g_ref[...], NORM_EPS).astype(BF16)
    merged = None
    for i, (o_ref, w_ref) in enumerate(((oa_ref, wpa_ref), (ob_ref, wpb_ref), (oc_ref, wpc_ref))):
        gate = jax.nn.sigmoid(jnp.dot(h, wg_ref[:, i * D_MODEL:(i + 1) * D_MODEL],
                                      preferred_element_type=F32))
        term = gate * jnp.dot(o_ref[...].astype(BF16), w_ref[...], preferred_element_type=F32)
        merged = term if merged is None else merged + term
    out_ref[...] = x + jnp.dot(merged.astype(BF16), wo_ref[...], preferred_element_type=F32)


def merge(x, oa, ob, oc, P, layer):
    n = x.shape[0]
    tm = min(ROW_TILE, n)
    row = lambda c: pl.BlockSpec((tm, c), lambda i: (i, 0))
    return pl.pallas_call(
        _merge_kernel,
        grid=(n // tm,),
        in_specs=[row(D_MODEL), row(W_A), row(W_B), row(W_C), _layer((1, D_MODEL), layer),
                  _layer((D_MODEL, 3 * D_MODEL), layer), _layer((W_A, D_MODEL), layer),
                  _layer((W_B, D_MODEL), layer), _layer((W_C, D_MODEL), layer),
                  _layer((D_MODEL, D_MODEL), layer)],
        out_specs=row(D_MODEL),
        out_shape=jax.ShapeDtypeStruct((n, D_MODEL), F32),
        compiler_params=_cparams(("parallel",)),
        name="merge",
    )(x, oa, ob, oc, P['norm1'], P['w_gate'], P['w_pa'], P['w_pb'], P['w_pc'], P['w_o'])


def _ffn_kernel(x_ref, c0_ref, g_ref, win_ref, cw_ref, cb_ref, wout_ref, out_ref, cT_ref, carry_scr):
    sb, tl = x_ref.shape[0], x_ref.shape[1]
    ti = pl.program_id(1)

    @pl.when(ti == 0)
    def _():
        carry_scr[...] = c0_ref[...]

    x = x_ref[...].reshape(sb * tl, D_MODEL)
    h = _rms(x, g_ref[...], NORM_EPS).astype(BF16)
    u = jnp.dot(h, win_ref[:, 0:D_FF], preferred_element_type=F32)
    gate = jnp.dot(h, win_ref[:, D_FF:2 * D_FF], preferred_element_type=F32)
    t = lax.broadcasted_iota(jnp.int32, (sb, tl, D_FF), 1).reshape(sb * tl, D_FF)
    bc = lambda j: jnp.broadcast_to(carry_scr[:, j:j + 1, :], (sb, tl, D_FF)).reshape(sb * tl, D_FF)
    prev1 = jnp.where(t == 0, bc(1), pltpu.roll(u, 1, axis=0))
    prev2 = jnp.where(t == 0, bc(0), jnp.where(t == 1, bc(1), pltpu.roll(u, 2, axis=0)))
    u3 = u.reshape(sb, tl, D_FF)
    carry_scr[...] = u3[:, tl - 2:tl, :]
    uc = cb_ref[...] + prev2 * cw_ref[0:1, :]
    uc = uc + prev1 * cw_ref[1:2, :]
    uc = uc + u * cw_ref[2:3, :]
    act = jax.nn.gelu(uc) * gate
    y = x + jnp.dot(act.astype(BF16), wout_ref[...], preferred_element_type=F32)
    out_ref[...] = y.reshape(sb, tl, D_MODEL)

    @pl.when(ti == pl.num_programs(1) - 1)
    def _():
        cT_ref[...] = carry_scr[...]


def ffn(x, conv0, P, layer, sb, tl):
    nseq, L, _ = x.shape
    blk = pl.BlockSpec((sb, tl, D_MODEL), lambda s, t: (s, t, 0))
    cst = pl.BlockSpec((sb, CONV_W - 1, D_FF), lambda s, t: (s, 0, 0))
    return pl.pallas_call(
        _ffn_kernel,
        grid=(nseq // sb, L // tl),
        in_specs=[blk, cst, _layer((1, D_MODEL), layer), _layer((D_MODEL, 2 * D_FF), layer),
                  _layer((CONV_W, D_FF), layer), _layer((1, D_FF), layer), _layer((D_FF, D_MODEL), layer)],
        out_specs=[blk, cst],
        out_shape=[jax.ShapeDtypeStruct((nseq, L, D_MODEL), F32),
                   jax.ShapeDtypeStruct((nseq, CONV_W - 1, D_FF), F32)],
        scratch_shapes=[pltpu.VMEM((sb, CONV_W - 1, D_FF), F32)],
        compiler_params=_cparams(("parallel", "arbitrary")),
        name="ffn",
    )(x, conv0, P['norm2'], P['w_ffn_in'], P['conv_w'], P['conv_b'], P['w_ffn_out'])


def _norm_kernel(x_ref, g_ref, out_ref):
    out_ref[...] = _rms(x_ref[...], g_ref[...], NORM_EPS)


def final_norm(x, gain):
    n = x.shape[0]
    tm = min(ROW_TILE, n)
    row = pl.BlockSpec((tm, D_MODEL), lambda i: (i, 0))
    return pl.pallas_call(
        _norm_kernel,
        grid=(n // tm,),
        in_specs=[row, _full((1, D_MODEL))],
        out_specs=row,
        out_shape=jax.ShapeDtypeStruct((n, D_MODEL), F32),
        compiler_params=_cparams(("parallel",)),
        name="final_norm",
    )(x, gain)


def _prepare_params(W):
    P = {}
    w_in = W['w_in']
    P['w_a'] = w_in[:, :, 0:A_COLS].astype(BF16)
    P['w_b'] = w_in[:, :, A_COLS:A_COLS + B_COLS].astype(BF16)
    P['w_c'] = w_in[:, :, A_COLS + B_COLS:A_COLS + B_COLS + C_COLS].astype(BF16)
    P['w_gate'] = w_in[:, :, A_COLS + B_COLS + C_COLS:].astype(BF16)
    for name in ('w_pa', 'w_pb', 'w_pc', 'w_o', 'w_ffn_in', 'w_ffn_out'):
        P[name] = W[name].astype(BF16)
    for name in ('norm1', 'norm2', 'mu_a', 'w0_a', 'a0_a', 'k_k', 'k_a', 'lnx_w', 'lnx_b', 'conv_b',
                 'lam_q1', 'lam_k1', 'lam_q2', 'lam_k2', 'subln'):
        P[name] = W[name][:, None, :]
    P['r_k'] = W['r_k'].reshape(DEPTH, 1, W_A)
    P['w2_a'] = jnp.concatenate([W['w2_a'], jnp.zeros((DEPTH, R_A, W_A), F32)], axis=1)
    P['a2_a'] = jnp.concatenate([jnp.zeros((DEPTH, R_W, W_A), F32), W['a2_a']], axis=1)
    P['g2_a'] = W['g2_a']
    P['conv_w'] = W['conv_w']
    return P


def _layer_group(x, states, pos0, attn_fn, P, layer, ffn_blk):
    nseq, L, _ = x.shape
    wkv0, shift0, ret0, conv0 = states
    flat = lambda z: z.reshape(nseq * L, z.shape[-1])
    seq = lambda z: z.reshape(nseq, L, z.shape[-1])
    pa, pb, q, k, v = in_proj(flat(x), P['norm1'], P['w_a'], P['w_b'], P['w_c'], layer)
    out_a, wkv_t, shift_t = rwkv(seq(pa), shift0, wkv0, P, layer)
    out_b, ret_t = retention(seq(pb), ret0, pos0)
    lam_init = 0.8 - 0.6 * math.exp(-0.3 * layer)
    q, k, v = seq(q), seq(k), seq(v)
    out_c = attn_fn(q, k, v, lam_init)
    x = merge(flat(x), flat(out_a), flat(out_b), flat(out_c), P, layer)
    x, conv_t = ffn(seq(x), conv0, P, layer, *ffn_blk)
    return x, (k, v, wkv_t, shift_t, ret_t, conv_t)


def kernel(x_prompt, x_sample, cache_k, cache_v, page_table, state_wkv, state_shift, state_ret,
           state_conv, meta, norm1, w_in, mu_a, w0_a, w2_a, a0_a, a2_a, g2_a, k_k, k_a, r_k,
           lnx_w, lnx_b, lam_q1, lam_k1, lam_q2, lam_k2, subln, w_pa, w_pb, w_pc, w_o, norm2,
           w_ffn_in, conv_w, conv_b, w_ffn_out, norm_f):
    W = dict(norm1=norm1, w_in=w_in, mu_a=mu_a, w0_a=w0_a, w2_a=w2_a, a0_a=a0_a, a2_a=a2_a,
             g2_a=g2_a, k_k=k_k, k_a=k_a, r_k=r_k, lnx_w=lnx_w, lnx_b=lnx_b, lam_q1=lam_q1,
             lam_k1=lam_k1, lam_q2=lam_q2, lam_k2=lam_k2, subln=subln, w_pa=w_pa, w_pb=w_pb,
             w_pc=w_pc, w_o=w_o, norm2=norm2, w_ffn_in=w_ffn_in, conv_w=conv_w, conv_b=conv_b,
             w_ffn_out=w_ffn_out)
    P = _prepare_params(W)
    b, t_main, _ = x_prompt.shape
    bd, ds, _ = x_sample.shape
    past = page_table.shape[1] * PAGE_SIZE

    x_meta = jnp.broadcast_to(meta[None].astype(F32), (b, N_META, D_MODEL))
    x_main = x_prompt
    x_samp = x_sample
    zeros = lambda *s: jnp.zeros(s, F32)
    meta_states = (zeros(b, H_A, N_A, N_A), zeros(b, A_COLS), zeros(b, H_B, DK_B, DV_B),
                   zeros(b, CONV_W - 1, D_FF))
    outs_p = [[] for _ in range(6)]
    outs_s = [[] for _ in range(6)]
    for l in range(DEPTH):
        x_meta, st_meta = _layer_group(
            x_meta, meta_states, 0,
            lambda q, k, v, li: attn_prompt(q, k, v, None, P, l, li), P, l, (b, N_META))
        km, vm = st_meta[0], st_meta[1]
        x_main, st_main = _layer_group(
            x_main, st_meta[2:], N_META,
            lambda q, k, v, li: attn_prompt(q, k, v, (km, vm), P, l, li), P, l,
            (1, min(ROW_TILE, t_main)))
        x_samp, st_samp = _layer_group(
            x_samp, (state_wkv[l], state_shift[l], state_ret[l], state_conv[l]), past,
            lambda q, k, v, li: attn_sample(q, k, v, cache_k, cache_v, page_table, P, l, li), P, l,
            (bd, ds))
        heads = lambda z: z.reshape(z.shape[0], z.shape[1], H_C, DV_C)
        outs_p[0].append(jnp.concatenate([heads(km), heads(st_main[0])], axis=1))
        outs_p[1].append(jnp.concatenate([heads(vm), heads(st_main[1])], axis=1))
        for i in range(2, 6):
            outs_p[i].append(st_main[i])
        outs_s[0].append(heads(st_samp[0]))
        outs_s[1].append(heads(st_samp[1]))
        for i in range(2, 6):
            outs_s[i].append(st_samp[i])
    gain_f = norm_f[None, :]
    y_prompt = final_norm(x_main.reshape(b * t_main, D_MODEL), gain_f).reshape(b, t_main, D_MODEL)
    y_sample = final_norm(x_samp.reshape(bd * ds, D_MODEL), gain_f).reshape(bd, ds, D_MODEL)
    stack = lambda lst: jnp.stack(lst, 0)
    return (y_prompt, y_sample, *[stack(o) for o in outs_p], *[stack(o) for o in outs_s])
```

```python
import math

import jax
import jax.numpy as jnp
from jax import lax
from jax.experimental import pallas as pl
from jax.experimental.pallas import tpu as pltpu

F32 = jnp.float32
BF16 = jnp.bfloat16

D_MODEL = 1024
DEPTH = 4
N_META = 16
H_A, N_A = 8, 64
W_A = H_A * N_A
R_W, R_A, R_G = 64, 64, 128
H_B, DK_B, DV_B = 4, 64, 128
QK_B, W_B = H_B * DK_B, H_B * DV_B
ROPE_BASE = 10000.0
H_C, DH_C = 4, 64
DV_C = 2 * DH_C
W_C = H_C * DV_C
D_FF = 2816
CONV_W = 3
A_COLS = 3 * W_A + R_W + R_A + R_G
B_COLS = 2 * QK_B + 2 * W_B
C_COLS = 3 * W_C
NORM_EPS = 1e-6
GN_EPS_A = 64e-5
SUBLN_EPS = 1e-5
NEG_INF = -1e30
PAGE_SIZE = 128

LANES = 128
VMEM_LIMIT = 52 * 1024 * 1024
ROW_TILE = 512
RWKV_SEQS = 4
RWKV_TILE = 256
RET_CHUNK = 128
ATTN_BLOCK = 512
ATTN_ROWS = 256
PAGES_PER_STEP = 8

HIGHEST = lax.Precision.HIGHEST


def _cparams(sem):
    return pltpu.CompilerParams(dimension_semantics=sem, vmem_limit_bytes=VMEM_LIMIT)


def _rms(x, gain, eps):
    return x * lax.rsqrt(jnp.mean(x * x, axis=-1, keepdims=True) + eps) * gain


def _bdot(a, b):
    return jnp.dot(a.astype(BF16), b.astype(BF16), preferred_element_type=F32)


def _bdot_nt(a, b):
    return lax.dot_general(a.astype(BF16), b.astype(BF16), (((1,), (1,)), ((), ())),
                           preferred_element_type=F32)


def _bdot_tn(a, b):
    return lax.dot_general(a.astype(BF16), b.astype(BF16), (((0,), (0,)), ((), ())),
                           preferred_element_type=F32)


def _fdot(a, b):
    return jnp.dot(a, b, precision=HIGHEST, preferred_element_type=F32)


def _full(shape):
    return pl.BlockSpec(shape, lambda *_: (0,) * len(shape))


def _layer(shape, layer):
    return pl.BlockSpec((None,) + shape, lambda *_: (layer,) + (0,) * len(shape))


def _in_proj_kernel(x_ref, g_ref, wa_ref, wb_ref, wc_ref, pa_ref, pb_ref, q_ref, k_ref, v_ref):
    h = _rms(x_ref[...], g_ref[...], NORM_EPS).astype(BF16)
    pa_ref[...] = jnp.dot(h, wa_ref[...], preferred_element_type=F32)
    pb_ref[...] = jnp.dot(h, wb_ref[...], preferred_element_type=F32)
    q_ref[...] = jnp.dot(h, wc_ref[:, 0:W_C], preferred_element_type=F32)
    k_ref[...] = jnp.dot(h, wc_ref[:, W_C:2 * W_C], preferred_element_type=F32)
    v_ref[...] = jnp.dot(h, wc_ref[:, 2 * W_C:3 * W_C], preferred_element_type=F32)


def in_proj(x, gain, wa, wb, wc, layer):
    n = x.shape[0]
    tm = min(ROW_TILE, n)
    row = lambda c: pl.BlockSpec((tm, c), lambda i: (i, 0))
    return pl.pallas_call(
        _in_proj_kernel,
        grid=(n // tm,),
        in_specs=[row(D_MODEL), _layer((1, D_MODEL), layer), _layer((D_MODEL, A_COLS), layer),
                  _layer((D_MODEL, B_COLS), layer), _layer((D_MODEL, C_COLS), layer)],
        out_specs=[row(A_COLS), row(B_COLS), row(W_C), row(W_C), row(W_C)],
        out_shape=[jax.ShapeDtypeStruct((n, c), F32) for c in (A_COLS, B_COLS, W_C, W_C, W_C)],
        compiler_params=_cparams(("parallel",)),
        name="in_proj",
    )(x, gain, wa, wb, wc)


QUAD = 4 * N_A


def _block_ones(n, dtype):
    r = lax.broadcasted_iota(jnp.int32, (n, n), 0) // N_A
    c = lax.broadcasted_iota(jnp.int32, (n, n), 1) // N_A
    return (r == c).astype(dtype)


def _head_sum(x, ones):
    hi = x.astype(BF16)
    lo = (x - hi.astype(F32)).astype(BF16)
    return (jnp.dot(hi, ones, preferred_element_type=F32)
            + jnp.dot(lo, ones, preferred_element_type=F32))


def _rwkv_kernel(pa_ref, shift0_ref, s0_ref, mu_ref, w0_ref, w2_ref, a0_ref, a2_ref, g2_ref,
                 kk_ref, ka_ref, rk_ref, lnw_ref, lnb_ref,
                 out_ref, sT_ref, shiftT_ref,
                 s_scr, carry_scr, r_scr, y_scr, k_scr, v_scr, d_scr, al_scr, be_scr, g_scr, bonus_scr):
    nb, lt = pa_ref.shape[0], pa_ref.shape[1]
    nquad = W_A // QUAD
    ti = pl.program_id(1)

    @pl.when(ti == 0)
    def _():
        s_scr[...] = s0_ref[...].reshape(nb * nquad, N_A, QUAD)
        carry_scr[...] = shift0_ref[...]

    ones = _block_ones(W_A, BF16)
    row = lax.broadcasted_iota(jnp.int32, (lt, A_COLS), 0)
    for b in range(nb):
        x = pa_ref[b]
        prev = jnp.where(row == 0, carry_scr[b], pltpu.roll(x, 1, axis=0))
        carry_scr[b] = x[lt - 1:lt, :]
        xm = x + (prev - x) * mu_ref[...]
        r = xm[:, 0:W_A]
        k = xm[:, W_A:2 * W_A]
        v = xm[:, 2 * W_A:3 * W_A]
        lr = xm[:, 3 * W_A:3 * W_A + R_W + R_A]
        gd = xm[:, 3 * W_A + R_W + R_A:A_COLS]
        w = -jax.nn.softplus(-(w0_ref[...] + _fdot(jnp.tanh(lr), w2_ref[...]))) - 0.5
        a = jax.nn.sigmoid(a0_ref[...] + _fdot(lr, a2_ref[...]))
        kk = k * kk_ref[...]
        kk = kk * lax.rsqrt(jnp.maximum(_head_sum(kk * kk, ones), 1e-24))
        k = k * (1.0 + (a - 1.0) * ka_ref[...])
        g_scr[b] = _fdot(jax.nn.sigmoid(gd), g2_ref[...])
        bonus_scr[b] = _head_sum(r * k * rk_ref[...], ones) * v
        for q in range(nquad):
            i, cols = b * nquad + q, slice(q * QUAD, (q + 1) * QUAD)
            r_scr[i, 0:8, :] = jnp.zeros((8, QUAD), F32)
            r_scr[i, 8:lt + 8, :] = r[:, cols]
            k_scr[i] = k[:, cols]
            v_scr[i] = v[:, cols]
            d_scr[i] = jnp.exp(-jnp.exp(w))[:, cols]
            al_scr[i] = -kk[:, cols]
            be_scr[i] = (kk * a)[:, cols]

    ones_q = _block_ones(QUAD, BF16)
    lane = lax.broadcasted_iota(jnp.int32, (N_A, QUAD), 1)
    sub = lax.broadcasted_iota(jnp.int32, (N_A, QUAD), 0)
    diag = (lane % N_A) == sub

    def to_row(z):
        return jnp.sum(jnp.where(diag, z, 0.0), axis=0, keepdims=True)

    def step(t, carry):
        for i in range(nb * nquad):
            s = s_scr[i]
            lhs = jnp.concatenate([(s * al_scr[i, pl.ds(t, 1), :]).astype(BF16),
                                   (s * r_scr[i, pl.ds(t + 7, 1), :]).astype(BF16),
                                   jnp.where(diag, v_scr[i, pl.ds(t, 1), :], 0.0).astype(BF16)], axis=0)
            o = jnp.dot(lhs, ones_q, preferred_element_type=F32)
            sa, y_prev, vcol = o[0:N_A], o[N_A:2 * N_A], o[2 * N_A:3 * N_A]
            s_scr[i] = (s * d_scr[i, pl.ds(t, 1), :] + sa * be_scr[i, pl.ds(t, 1), :]
                        + vcol * k_scr[i, pl.ds(t, 1), :])
            y_scr[i, pl.ds(t + 7, 1), :] = to_row(y_prev)
        return carry

    lax.fori_loop(0, lt, step, 0)

    for b in range(nb):
        for q in range(nquad):
            i = b * nquad + q
            o = jnp.dot((s_scr[i] * r_scr[i, lt + 7:lt + 8, :]).astype(BF16), ones_q,
                        preferred_element_type=F32)
            y_scr[i, lt + 7:lt + 8, :] = to_row(o)
        y = jnp.concatenate([y_scr[b * nquad + q, 8:lt + 8, :] for q in range(nquad)], axis=1)
        yc = y - _head_sum(y, ones) * (1.0 / N_A)
        yn = yc * lax.rsqrt(_head_sum(yc * yc, ones) * (1.0 / N_A) + GN_EPS_A)
        out_ref[b] = (yn * lnw_ref[...] + lnb_ref[...] + bonus_scr[b]) * g_scr[b]

    @pl.when(ti == pl.num_programs(1) - 1)
    def _():
        sT_ref[...] = s_scr[...].reshape(nb, nquad, N_A, QUAD)
        shiftT_ref[...] = carry_scr[...]


def _pack_wkv(s):
    n = s.shape[0]
    return s.reshape(n, H_A // 4, 4, N_A, N_A).transpose(0, 1, 3, 2, 4).reshape(n, H_A // 4, N_A, QUAD)


def _unpack_wkv(s):
    n = s.shape[0]
    return s.reshape(n, H_A // 4, N_A, 4, N_A).transpose(0, 1, 3, 2, 4).reshape(n, H_A, N_A, N_A)


def rwkv(pa, shift0, wkv0, P, layer):
    nseq, L, _ = pa.shape
    nb = RWKV_SEQS
    lt = min(RWKV_TILE, L)
    nquad = W_A // QUAD
    seq3 = lambda c: pl.BlockSpec((nb, lt, c), lambda s, t: (s, t, 0))
    vec = lambda name: _layer(P[name].shape[1:], layer)
    state = pl.BlockSpec((nb, nquad, N_A, QUAD), lambda s, t: (s, 0, 0, 0))
    slab = lambda rows: pltpu.VMEM((nb * nquad, rows, QUAD), F32)
    tile = lambda: pltpu.VMEM((nb, lt, W_A), F32)
    out, sT, shiftT = pl.pallas_call(
        _rwkv_kernel,
        grid=(nseq // nb, L // lt),
        in_specs=[seq3(A_COLS), pl.BlockSpec((nb, 1, A_COLS), lambda s, t: (s, 0, 0)), state,
                  vec('mu_a'), vec('w0_a'), vec('w2_a'), vec('a0_a'), vec('a2_a'), vec('g2_a'),
                  vec('k_k'), vec('k_a'), vec('r_k'), vec('lnx_w'), vec('lnx_b')],
        out_specs=[seq3(W_A), state, pl.BlockSpec((nb, 1, A_COLS), lambda s, t: (s, 0, 0))],
        out_shape=[jax.ShapeDtypeStruct((nseq, L, W_A), F32),
                   jax.ShapeDtypeStruct((nseq, nquad, N_A, QUAD), F32),
                   jax.ShapeDtypeStruct((nseq, 1, A_COLS), F32)],
        scratch_shapes=[pltpu.VMEM((nb * nquad, N_A, QUAD), F32), pltpu.VMEM((nb, 1, A_COLS), F32),
                        slab(lt + 8), slab(lt + 8)] + [slab(lt) for _ in range(5)] + [tile(), tile()],
        compiler_params=_cparams(("parallel", "arbitrary")),
        name="rwkv",
    )(pa, shift0[:, None, :], _pack_wkv(wkv0), P['mu_a'], P['w0_a'], P['w2_a'], P['a0_a'],
      P['a2_a'], P['g2_a'], P['k_k'], P['k_a'], P['r_k'], P['lnx_w'], P['lnx_b'])
    return out, _unpack_wkv(sT), shiftT[:, 0, :]


def _swap_halves(x):
    lane = lax.broadcasted_iota(jnp.int32, x.shape, 1)
    first = (lane % DK_B) < (DK_B // 2)
    return jnp.where(first, pltpu.roll(x, QK_B - DK_B // 2, axis=1), pltpu.roll(x, DK_B // 2, axis=1))


def _retention_kernel(pb_ref, r0_ref, cos_ref, sin_ref, dmat_ref, rowdec_ref, keydec_ref, chdec_ref,
                      out_ref, rT_ref, r_scr):
    ci = pl.program_id(1)

    @pl.when(ci == 0)
    def _():
        r_scr[...] = r0_ref[...]

    x = pb_ref[...]
    q = x[:, 0:QK_B]
    k = x[:, QK_B:2 * QK_B]
    q = q * cos_ref[...] + _swap_halves(q) * sin_ref[...]
    k = (k * cos_ref[...] + _swap_halves(k) * sin_ref[...]) * (DK_B ** -0.5)
    for h in range(H_B):
        qh = q[:, h * DK_B:(h + 1) * DK_B]
        kh = k[:, h * DK_B:(h + 1) * DK_B]
        vh = x[:, 2 * QK_B + h * DV_B:2 * QK_B + (h + 1) * DV_B]
        gh = x[:, 2 * QK_B + W_B + h * DV_B:2 * QK_B + W_B + (h + 1) * DV_B]
        rs = r_scr[h]
        s = _bdot_nt(qh, kh) * dmat_ref[h]
        o = _bdot(s, vh) + _bdot(qh, rs) * rowdec_ref[h]
        r_scr[h] = rs * chdec_ref[h] + _bdot_tn(kh * keydec_ref[h], vh)
        o = o * lax.rsqrt(jnp.mean(o * o, axis=-1, keepdims=True) + NORM_EPS)
        out_ref[:, h * DV_B:(h + 1) * DV_B] = o * jax.nn.silu(gh)

    @pl.when(ci == pl.num_programs(1) - 1)
    def _():
        rT_ref[...] = r_scr[...]


def _retention_tables(L, lc, pos0):
    lg = jnp.log1p(-jnp.exp2(-5.0 - jnp.arange(H_B, dtype=F32)))
    idx = jnp.arange(lc, dtype=F32)
    diff = idx[:, None] - idx[None, :]
    dmat = jnp.where(diff >= 0, jnp.exp(jnp.maximum(diff, 0.0)[None] * lg[:, None, None]), 0.0)
    rowdec = jnp.exp((idx[None, :] + 1.0) * lg[:, None])[..., None]
    keydec = jnp.exp((lc - 1.0 - idx)[None, :] * lg[:, None])[..., None]
    chdec = jnp.exp(lc * lg)[:, None, None]
    half = DK_B // 2
    inv = ROPE_BASE ** (-jnp.arange(half, dtype=F32) / half)
    ang = (pos0 + jnp.arange(L)).astype(F32)[:, None] * inv[None, :]
    cos, sin = jnp.cos(ang), jnp.sin(ang)
    cos_t = jnp.tile(jnp.concatenate([cos, cos], axis=1), (1, H_B))
    sin_t = jnp.tile(jnp.concatenate([-sin, sin], axis=1), (1, H_B))
    return cos_t, sin_t, dmat, rowdec, keydec, chdec


def retention(pb, ret0, pos0):
    nseq, L, _ = pb.shape
    lc = min(RET_CHUNK, L)
    cos_t, sin_t, dmat, rowdec, keydec, chdec = _retention_tables(L, lc, pos0)
    state = pl.BlockSpec((None, H_B, DK_B, DV_B), lambda s, c: (s, 0, 0, 0))
    tab = pl.BlockSpec((lc, QK_B), lambda s, c: (c, 0))
    out, rT = pl.pallas_call(
        _retention_kernel,
        grid=(nseq, L // lc),
        in_specs=[pl.BlockSpec((None, lc, B_COLS), lambda s, c: (s, c, 0)), state, tab, tab,
                  _full((H_B, lc, lc)), _full((H_B, lc, 1)), _full((H_B, lc, 1)), _full((H_B, 1, 1))],
        out_specs=[pl.BlockSpec((None, lc, W_B), lambda s, c: (s, c, 0)), state],
        out_shape=[jax.ShapeDtypeStruct((nseq, L, W_B), F32),
                   jax.ShapeDtypeStruct((nseq, H_B, DK_B, DV_B), F32)],
        scratch_shapes=[pltpu.VMEM((H_B, DK_B, DV_B), F32)],
        compiler_params=_cparams(("parallel", "arbitrary")),
        name="retention",
    )(pb, ret0, cos_t, sin_t, dmat, rowdec, keydec, chdec)
    return out, rT


def _lam(lq1, lk1, lq2, lk2, lam_init):
    return (jnp.exp(jnp.sum(lq1 * lk1, axis=-1, keepdims=True))
            - jnp.exp(jnp.sum(lq2 * lk2, axis=-1, keepdims=True)) + lam_init)


def _two_map_queries(q):
    lane = lax.broadcasted_iota(jnp.int32, q.shape, 1)
    q = q * (DH_C ** -0.5)
    return jnp.concatenate([jnp.where(lane < DH_C, q, 0.0), jnp.where(lane < DH_C, 0.0, q)],
                           axis=0).astype(BF16)


def _softmax_update(s, pv, m_ref, l_ref, acc_ref, rows=slice(None)):
    m_old = m_ref[rows]
    m_new = jnp.maximum(m_old, jnp.max(s, axis=-1, keepdims=True))
    alpha = jnp.exp(m_old - m_new)
    p = jnp.exp(s - m_new[:, 0:1])
    l_ref[rows] = alpha * l_ref[rows] + jnp.sum(p, axis=-1, keepdims=True)
    acc_ref[rows] = alpha * acc_ref[rows] + pv(p)
    m_ref[rows] = m_new


def _attn_prompt_kernel(lam_init, has_prefix, qi_ref, kj_ref, q_ref, k_ref, v_ref, *rest):
    if has_prefix:
        kp_ref, vp_ref, *rest = rest
    lq1, lk1, lq2, lk2, subln_ref, out_ref, qb_scr, m_scr, l_scr, acc_scr = rest
    step = pl.program_id(2)
    qi, kj = qi_ref[step], kj_ref[step]
    bq = q_ref.shape[0]

    def process(k, v, mask):
        k, v = k.astype(BF16), v.astype(BF16)
        rc = min(ATTN_ROWS, bq)
        for r0 in range(0, 2 * bq, rc):
            s = _bdot_nt(qb_scr[r0:r0 + rc], k)
            if mask:
                r = lax.broadcasted_iota(jnp.int32, s.shape, 0) + (r0 % bq)
                c = lax.broadcasted_iota(jnp.int32, s.shape, 1)
                s = jnp.where(c <= r, s, NEG_INF)
            _softmax_update(s, lambda p: _bdot(p, v), m_scr, l_scr, acc_scr, slice(r0, r0 + rc))

    @pl.when(kj == 0)
    def _():
        qb_scr[...] = _two_map_queries(q_ref[...])
        m_scr[...] = jnp.full(m_scr.shape, NEG_INF, F32)
        l_scr[...] = jnp.zeros(l_scr.shape, F32)
        acc_scr[...] = jnp.zeros(acc_scr.shape, F32)
        if has_prefix:
            process(kp_ref[...], vp_ref[...], False)

    @pl.when(kj < qi)
    def _():
        process(k_ref[...], v_ref[...], False)

    @pl.when(kj == qi)
    def _():
        process(k_ref[...], v_ref[...], True)
        lam = _lam(lq1[...], lk1[...], lq2[...], lk2[...], lam_init)
        o = acc_scr[0:bq] / l_scr[0:bq] - lam * (acc_scr[bq:2 * bq] / l_scr[bq:2 * bq])
        out_ref[...] = _rms(o, subln_ref[...], SUBLN_EPS) * (1.0 - lam_init)


def attn_prompt(q, k, v, prefix, P, layer, lam_init):
    nseq, L, _ = q.shape
    blk = min(ATTN_BLOCK, L)
    nblk = L // blk
    pairs = [(i, j) for i in range(nblk) for j in range(i + 1)]
    qi_tab = jnp.array([p[0] for p in pairs], jnp.int32)
    kj_tab = jnp.array([p[1] for p in pairs], jnp.int32)
    qspec = pl.BlockSpec((None, blk, DV_C), lambda b, h, s, qi, kj: (b, qi[s], h))
    kspec = pl.BlockSpec((None, blk, DV_C), lambda b, h, s, qi, kj: (b, kj[s], h))
    args, specs = [q, k, v], [qspec, kspec, kspec]
    if prefix is not None:
        npre = prefix[0].shape[1]
        pspec = pl.BlockSpec((None, npre, DV_C), lambda b, h, s, qi, kj: (b, 0, h))
        args += list(prefix)
        specs += [pspec, pspec]
    lspec = lambda c: pl.BlockSpec((None, 1, c), lambda b, h, s, qi, kj: (layer, 0, 0))
    for name in ('lam_q1', 'lam_k1', 'lam_q2', 'lam_k2'):
        args.append(P[name])
        specs.append(lspec(DH_C))
    args.append(P['subln'])
    specs.append(lspec(DV_C))
    wide = lambda dt: pltpu.VMEM((2 * blk, DV_C), dt)
    grid_spec = pltpu.PrefetchScalarGridSpec(
        num_scalar_prefetch=2,
        grid=(nseq, H_C, len(pairs)),
        in_specs=specs,
        out_specs=qspec,
        scratch_shapes=[wide(BF16), wide(F32), wide(F32), wide(F32)],
    )
    return pl.pallas_call(
        lambda *refs: _attn_prompt_kernel(lam_init, prefix is not None, *refs),
        grid_spec=grid_spec,
        out_shape=jax.ShapeDtypeStruct((nseq, L, W_C), F32),
        compiler_params=_cparams(("parallel", "parallel", "arbitrary")),
        name="attn_prompt",
    )(qi_tab, kj_tab, *args)


def _attn_sample_kernel(lam_init, npp, pt_ref, q_ref, kn_ref, vn_ref, lq1, lk1, lq2, lk2, subln_ref,
                        *rest):
    k_refs, v_refs = rest[:npp], rest[npp:2 * npp]
    out_ref, qb_scr, m_scr, l_scr, acc_scr = rest[2 * npp:]
    j = pl.program_id(1)
    nq = q_ref.shape[0]
    hr = 2 * nq
    head = lambda h: slice(h * DV_C, (h + 1) * DV_C)
    rows = lambda h: slice(h * hr, (h + 1) * hr)

    @pl.when(j == 0)
    def _():
        for h in range(H_C):
            qb_scr[rows(h), :] = _two_map_queries(q_ref[:, head(h)])
        m_scr[...] = jnp.full(m_scr.shape, NEG_INF, F32)
        l_scr[...] = jnp.zeros(l_scr.shape, F32)
        acc_scr[...] = jnp.zeros(acc_scr.shape, F32)

    of_head = lambda ref, h: ref[pl.ds(h, PAGE_SIZE, stride=H_C), :]
    s = jnp.concatenate(
        [jnp.concatenate([_bdot_nt(qb_scr[rows(h), :], of_head(k_refs[i], h)) for i in range(npp)], axis=1)
         for h in range(H_C)], axis=0)

    def pv_pages(p):
        outs = []
        for h in range(H_C):
            tot = None
            for i in range(npp):
                t = _bdot(p[rows(h), i * PAGE_SIZE:(i + 1) * PAGE_SIZE], of_head(v_refs[i], h))
                tot = t if tot is None else tot + t
            outs.append(tot)
        return jnp.concatenate(outs, axis=0)

    _softmax_update(s, pv_pages, m_scr, l_scr, acc_scr)

    @pl.when(j == pl.num_programs(1) - 1)
    def _():
        sn = jnp.concatenate([_bdot_nt(qb_scr[rows(h), :], kn_ref[:, head(h)]) for h in range(H_C)], axis=0)
        r = lax.broadcasted_iota(jnp.int32, sn.shape, 0)
        c = lax.broadcasted_iota(jnp.int32, sn.shape, 1)
        sn = jnp.where(c <= r % nq, sn, NEG_INF)
        _softmax_update(
            sn, lambda p: jnp.concatenate([_bdot(p[rows(h), :], vn_ref[:, head(h)]) for h in range(H_C)], axis=0),
            m_scr, l_scr, acc_scr)
        lam = _lam(lq1[...], lk1[...], lq2[...], lk2[...], lam_init)
        o_all = acc_scr[...] / l_scr[...]
        for h in range(H_C):
            o = o_all[h * hr:h * hr + nq] - lam * o_all[h * hr + nq:(h + 1) * hr]
            out_ref[:, head(h)] = _rms(o, subln_ref[...], SUBLN_EPS) * (1.0 - lam_init)


def attn_sample(q, k, v, cache_k, cache_v, page_table, P, layer, lam_init):
    nseq, nq, _ = q.shape
    n_pages = page_table.shape[1]
    npp = PAGES_PER_STEP
    seq = pl.BlockSpec((None, nq, W_C), lambda b, j, pt: (b, 0, 0))

    n_pool = cache_k.shape[1]
    cache_k = cache_k.reshape(DEPTH, n_pool, PAGE_SIZE * H_C, DV_C)
    cache_v = cache_v.reshape(DEPTH, n_pool, PAGE_SIZE * H_C, DV_C)

    def page(i):
        return pl.BlockSpec((None, None, PAGE_SIZE * H_C, DV_C),
                            lambda b, j, pt: (layer, pt[b * n_pages + j * npp + i], 0, 0))

    lspec = lambda c: pl.BlockSpec((None, 1, c), lambda b, j, pt: (layer, 0, 0))
    rows = H_C * 2 * nq
    grid_spec = pltpu.PrefetchScalarGridSpec(
        num_scalar_prefetch=1,
        grid=(nseq, n_pages // npp),
        in_specs=[seq, seq, seq, lspec(DH_C), lspec(DH_C), lspec(DH_C), lspec(DH_C), lspec(DV_C)]
                 + [page(i) for i in range(npp)] + [page(i) for i in range(npp)],
        out_specs=seq,
        scratch_shapes=[pltpu.VMEM((rows, DV_C), BF16), pltpu.VMEM((rows, DV_C), F32),
                        pltpu.VMEM((rows, DV_C), F32), pltpu.VMEM((rows, DV_C), F32)],
    )
    return pl.pallas_call(
        lambda *refs: _attn_sample_kernel(lam_init, npp, *refs),
        grid_spec=grid_spec,
        out_shape=jax.ShapeDtypeStruct((nseq, nq, W_C), F32),
        compiler_params=_cparams(("parallel", "arbitrary")),
        name="attn_sample",
    )(page_table.reshape(-1), q, k, v, P['lam_q1'], P['lam_k1'], P['lam_q2'], P['lam_k2'], P['subln'],
      *([cache_k] * npp), *([cache_v] * npp))


def _merge_kernel(x_ref, oa_ref, ob_ref, oc_ref, g_ref, wg_ref, wpa_ref, wpb_ref, wpc_ref, wo_ref,
                  out_ref):
    x = x_ref[...]
    h = _rms(x, g_ref[...], NORM_EPS).astype(BF16)
    merged = None
    for i, (o_ref, w_ref) in enumerate(((oa_ref, wpa_ref), (ob_ref, wpb_ref), (oc_ref, wpc_ref))):
        gate = jax.nn.sigmoid(jnp.dot(h, wg_ref[:, i * D_MODEL:(i + 1) * D_MODEL],
                                      preferred_element_type=F32))
        term = gate * jnp.dot(o_ref[...].astype(BF16), w_ref[...], preferred_element_type=F32)
        merged = term if merged is None else merged + term
    out_ref[...] = x + jnp.dot(merged.astype(BF16), wo_ref[...], preferred_element_type=F32)


def merge(x, oa, ob, oc, P, layer):
    n = x.shape[0]
    tm = min(ROW_TILE, n)
    row = lambda c: pl.BlockSpec((tm, c), lambda i: (i, 0))
    return pl.pallas_call(
        _merge_kernel,
        grid=(n // tm,),
        in_specs=[row(D_MODEL), row(W_A), row(W_B), row(W_C), _layer((1, D_MODEL), layer),
                  _layer((D_MODEL, 3 * D_MODEL), layer), _layer((W_A, D_MODEL), layer),
                  _layer((W_B, D_MODEL), layer), _layer((W_C, D_MODEL), layer),
                  _layer((D_MODEL, D_MODEL), layer)],
        out_specs=row(D_MODEL),
        out_shape=jax.ShapeDtypeStruct((n, D_MODEL), F32),
        compiler_params=_cparams(("parallel",)),
        name="merge",
    )(x, oa, ob, oc, P['norm1'], P['w_gate'], P['w_pa'], P['w_pb'], P['w_pc'], P['w_o'])


def _ffn_kernel(x_ref, c0_ref, g_ref, win_ref, cw_ref, cb_ref, wout_ref, out_ref, cT_ref, carry_scr):
    sb, tl = x_ref.shape[0], x_ref.shape[1]
    ti = pl.program_id(1)

    @pl.when(ti == 0)
    def _():
        carry_scr[...] = c0_ref[...]

    x = x_ref[...].reshape(sb * tl, D_MODEL)
    h = _rms(x, g_ref[...], NORM_EPS).astype(BF16)
    u = jnp.dot(h, win_ref[:, 0:D_FF], preferred_element_type=F32)
    gate = jnp.dot(h, win_ref[:, D_FF:2 * D_FF], preferred_element_type=F32)
    t = lax.broadcasted_iota(jnp.int32, (sb, tl, D_FF), 1).reshape(sb * tl, D_FF)
    bc = lambda j: jnp.broadcast_to(carry_scr[:, j:j + 1, :], (sb, tl, D_FF)).reshape(sb * tl, D_FF)
    prev1 = jnp.where(t == 0, bc(1), pltpu.roll(u, 1, axis=0))
    prev2 = jnp.where(t == 0, bc(0), jnp.where(t == 1, bc(1), pltpu.roll(u, 2, axis=0)))
    u3 = u.reshape(sb, tl, D_FF)
    carry_scr[...] = u3[:, tl - 2:tl, :]
    uc = cb_ref[...] + prev2 * cw_ref[0:1, :]
    uc = uc + prev1 * cw_ref[1:2, :]
    uc = uc + u * cw_ref[2:3, :]
    act = jax.nn.gelu(uc) * gate
    y = x + jnp.dot(act.astype(BF16), wout_ref[...], preferred_element_type=F32)
    out_ref[...] = y.reshape(sb, tl, D_MODEL)

    @pl.when(ti == pl.num_programs(1) - 1)
    def _():
        cT_ref[...] = carry_scr[...]


def ffn(x, conv0, P, layer, sb, tl):
    nseq, L, _ = x.shape
    blk = pl.BlockSpec((sb, tl, D_MODEL), lambda s, t: (s, t, 0))
    cst = pl.BlockSpec((sb, CONV_W - 1, D_FF), lambda s, t: (s, 0, 0))
    return pl.pallas_call(
        _ffn_kernel,
        grid=(nseq // sb, L // tl),
        in_specs=[blk, cst, _layer((1, D_MODEL), layer), _layer((D_MODEL, 2 * D_FF), layer),
                  _layer((CONV_W, D_FF), layer), _layer((1, D_FF), layer), _layer((D_FF, D_MODEL), layer)],
        out_specs=[blk, cst],
        out_shape=[jax.ShapeDtypeStruct((nseq, L, D_MODEL), F32),
                   jax.ShapeDtypeStruct((nseq, CONV_W - 1, D_FF), F32)],
        scratch_shapes=[pltpu.VMEM((sb, CONV_W - 1, D_FF), F32)],
        compiler_params=_cparams(("parallel", "arbitrary")),
        name="ffn",
    )(x, conv0, P['norm2'], P['w_ffn_in'], P['conv_w'], P['conv_b'], P['w_ffn_out'])


def _norm_kernel(x_ref, g_ref, out_ref):
    out_ref[...] = _rms(x_ref[...], g_ref[...], NORM_EPS)


def final_norm(x, gain):
    n = x.shape[0]
    tm = min(ROW_TILE, n)
    row = pl.BlockSpec((tm, D_MODEL), lambda i: (i, 0))
    return pl.pallas_call(
        _norm_kernel,
        grid=(n // tm,),
        in_specs=[row, _full((1, D_MODEL))],
        out_specs=row,
        out_shape=jax.ShapeDtypeStruct((n, D_MODEL), F32),
        compiler_params=_cparams(("parallel",)),
        name="final_norm",
    )(x, gain)


def _prepare_params(W):
    P = {}
    w_in = W['w_in']
    P['w_a'] = w_in[:, :, 0:A_COLS].astype(BF16)
    P['w_b'] = w_in[:, :, A_COLS:A_COLS + B_COLS].astype(BF16)
    P['w_c'] = w_in[:, :, A_COLS + B_COLS:A_COLS + B_COLS + C_COLS].astype(BF16)
    P['w_gate'] = w_in[:, :, A_COLS + B_COLS + C_COLS:].astype(BF16)
    for name in ('w_pa', 'w_pb', 'w_pc', 'w_o', 'w_ffn_in', 'w_ffn_out'):
        P[name] = W[name].astype(BF16)
    for name in ('norm1', 'norm2', 'mu_a', 'w0_a', 'a0_a', 'k_k', 'k_a', 'lnx_w', 'lnx_b', 'conv_b',
                 'lam_q1', 'lam_k1', 'lam_q2', 'lam_k2', 'subln'):
        P[name] = W[name][:, None, :]
    P['r_k'] = W['r_k'].reshape(DEPTH, 1, W_A)
    P['w2_a'] = jnp.concatenate([W['w2_a'], jnp.zeros((DEPTH, R_A, W_A), F32)], axis=1)
    P['a2_a'] = jnp.concatenate([jnp.zeros((DEPTH, R_W, W_A), F32), W['a2_a']], axis=1)
    P['g2_a'] = W['g2_a']
    P['conv_w'] = W['conv_w']
    return P


def _layer_group(x, states, pos0, attn_fn, P, layer, ffn_blk):
    nseq, L, _ = x.shape
    wkv0, shift0, ret0, conv0 = states
    flat = lambda z: z.reshape(nseq * L, z.shape[-1])
    seq = lambda z: z.reshape(nseq, L, z.shape[-1])
    pa, pb, q, k, v = in_proj(flat(x), P['norm1'], P['w_a'], P['w_b'], P['w_c'], layer)
    out_a, wkv_t, shift_t = rwkv(seq(pa), shift0, wkv0, P, layer)
    out_b, ret_t = retention(seq(pb), ret0, pos0)
    lam_init = 0.8 - 0.6 * math.exp(-0.3 * layer)
    q, k, v = seq(q), seq(k), seq(v)
    out_c = attn_fn(q, k, v, lam_init)
    x = merge(flat(x), flat(out_a), flat(out_b), flat(out_c), P, layer)
    x, conv_t = ffn(seq(x), conv0, P, layer, *ffn_blk)
    return x, (k, v, wkv_t, shift_t, ret_t, conv_t)


def kernel(x_prompt, x_sample, cache_k, cache_v, page_table, state_wkv, state_shift, state_ret,
           state_conv, meta, norm1, w_in, mu_a, w0_a, w2_a, a0_a, a2_a, g2_a, k_k, k_a, r_k,
           lnx_w, lnx_b, lam_q1, lam_k1, lam_q2, lam_k2, subln, w_pa, w_pb, w_pc, w_o, norm2,
           w_ffn_in, conv_w, conv_b, w_ffn_out, norm_f):
    W = dict(norm1=norm1, w_in=w_in, mu_a=mu_a, w0_a=w0_a, w2_a=w2_a, a0_a=a0_a, a2_a=a2_a,
             g2_a=g2_a, k_k=k_k, k_a=k_a, r_k=r_k, lnx_w=lnx_w, lnx_b=lnx_b, lam_q1=lam_q1,
             lam_k1=lam_k1, lam_q2=lam_q2, lam_k2=lam_k2, subln=subln, w_pa=w_pa, w_pb=w_pb,
             w_pc=w_pc, w_o=w_o, norm2=norm2, w_ffn_in=w_ffn_in, conv_w=conv_w, conv_b=conv_b,
             w_ffn_out=w_ffn_out)
    P = _prepare_params(W)
    b, t_main, _ = x_prompt.shape
    bd, ds, _ = x_sample.shape
    past = page_table.shape[1] * PAGE_SIZE

    x_meta = jnp.broadcast_to(meta[None].astype(F32), (b, N_META, D_MODEL))
    x_main = x_prompt
    x_samp = x_sample
    zeros = lambda *s: jnp.zeros(s, F32)
    meta_states = (zeros(b, H_A, N_A, N_A), zeros(b, A_COLS), zeros(b, H_B, DK_B, DV_B),
                   zeros(b, CONV_W - 1, D_FF))
    outs_p = [[] for _ in range(6)]
    outs_s = [[] for _ in range(6)]
    for l in range(DEPTH):
        x_meta, st_meta = _layer_group(
            x_meta, meta_states, 0,
            lambda q, k, v, li: attn_prompt(q, k, v, None, P, l, li), P, l, (b, N_META))
        km, vm = st_meta[0], st_meta[1]
        x_main, st_main = _layer_group(
            x_main, st_meta[2:], N_META,
            lambda q, k, v, li: attn_prompt(q, k, v, (km, vm), P, l, li), P, l,
            (1, min(ROW_TILE, t_main)))
        x_samp, st_samp = _layer_group(
            x_samp, (state_wkv[l], state_shift[l], state_ret[l], state_conv[l]), past,
            lambda q, k, v, li: attn_sample(q, k, v, cache_k, cache_v, page_table, P, l, li), P, l,
            (bd, ds))
        heads = lambda z: z.reshape(z.shape[0], z.shape[1], H_C, DV_C)
        outs_p[0].append(jnp.concatenate([heads(km), heads(st_main[0])], axis=1))
        outs_p[1].append(jnp.concatenate([heads(vm), heads(st_main[1])], axis=1))
        for i in range(2, 6):
            outs_p[i].append(st_main[i])
        outs_s[0].append(heads(st_samp[0]))
        outs_s[1].append(heads(st_samp[1]))
        for i in range(2, 6):
            outs_s[i].append(st_samp[i])
    gain_f = norm_f[None, :]
    y_prompt = final_norm(x_main.reshape(b * t_main, D_MODEL), gain_f).reshape(b, t_main, D_MODEL)
    y_sample = final_norm(x_samp.reshape(bd * ds, D_MODEL), gain_f).reshape(bd, ds, D_MODEL)
    stack = lambda lst: jnp.stack(lst, 0)
    return (y_prompt, y_sample, *[stack(o) for o in outs_p], *[stack(o) for o in outs_s])
```

```python
import math

import jax
import jax.numpy as jnp
from jax import lax
from jax.experimental import pallas as pl
from jax.experimental.pallas import tpu as pltpu

F32 = jnp.float32
BF16 = jnp.bfloat16

D_MODEL = 1024
DEPTH = 4
N_META = 16
H_A, N_A = 8, 64
W_A = H_A * N_A
R_W, R_A, R_G = 64, 64, 128
H_B, DK_B, DV_B = 4, 64, 128
QK_B, W_B = H_B * DK_B, H_B * DV_B
ROPE_BASE = 10000.0
H_C, DH_C = 4, 64
DV_C = 2 * DH_C
W_C = H_C * DV_C
D_FF = 2816
CONV_W = 3
A_COLS = 3 * W_A + R_W + R_A + R_G
B_COLS = 2 * QK_B + 2 * W_B
C_COLS = 3 * W_C
NORM_EPS = 1e-6
GN_EPS_A = 64e-5
SUBLN_EPS = 1e-5
NEG_INF = -1e30
PAGE_SIZE = 128

LANES = 128
VMEM_LIMIT = 52 * 1024 * 1024
ROW_TILE = 512
RWKV_SEQS = 4
RWKV_TILE = 128
RWKV_GROUP = 2
RWKV_UNROLL = 8
RET_CHUNK = 128
ATTN_BLOCK = 512
ATTN_ROWS = 256
PAGES_PER_STEP = 8

HIGHEST = lax.Precision.HIGHEST


def _cparams(sem):
    return pltpu.CompilerParams(dimension_semantics=sem, vmem_limit_bytes=VMEM_LIMIT)


def _rms(x, gain, eps):
    return x * lax.rsqrt(jnp.mean(x * x, axis=-1, keepdims=True) + eps) * gain


def _bdot(a, b):
    return jnp.dot(a.astype(BF16), b.astype(BF16), preferred_element_type=F32)


def _bdot_nt(a, b):
    return lax.dot_general(a.astype(BF16), b.astype(BF16), (((1,), (1,)), ((), ())),
                           preferred_element_type=F32)


def _bdot_tn(a, b):
    return lax.dot_general(a.astype(BF16), b.astype(BF16), (((0,), (0,)), ((), ())),
                           preferred_element_type=F32)


def _fdot(a, b):
    return jnp.dot(a, b, precision=HIGHEST, preferred_element_type=F32)


def _full(shape):
    return pl.BlockSpec(shape, lambda *_: (0,) * len(shape))


def _layer(shape, layer):
    return pl.BlockSpec((None,) + shape, lambda *_: (layer,) + (0,) * len(shape))


def _in_proj_kernel(x_ref, g_ref, wa_ref, wb_ref, wc_ref, pa_ref, pb_ref, q_ref, k_ref, v_ref):
    h = _rms(x_ref[...], g_ref[...], NORM_EPS).astype(BF16)
    pa_ref[...] = jnp.dot(h, wa_ref[...], preferred_element_type=F32)
    pb_ref[...] = jnp.dot(h, wb_ref[...], preferred_element_type=F32)
    q_ref[...] = jnp.dot(h, wc_ref[:, 0:W_C], preferred_element_type=F32)
    k_ref[...] = jnp.dot(h, wc_ref[:, W_C:2 * W_C], preferred_element_type=F32)
    v_ref[...] = jnp.dot(h, wc_ref[:, 2 * W_C:3 * W_C], preferred_element_type=F32)


def in_proj(x, gain, wa, wb, wc, layer):
    n = x.shape[0]
    tm = min(ROW_TILE, n)
    row = lambda c: pl.BlockSpec((tm, c), lambda i: (i, 0))
    return pl.pallas_call(
        _in_proj_kernel,
        grid=(n // tm,),
        in_specs=[row(D_MODEL), _layer((1, D_MODEL), layer), _layer((D_MODEL, A_COLS), layer),
                  _layer((D_MODEL, B_COLS), layer), _layer((D_MODEL, C_COLS), layer)],
        out_specs=[row(A_COLS), row(B_COLS), row(W_C), row(W_C), row(W_C)],
        out_shape=[jax.ShapeDtypeStruct((n, c), F32) for c in (A_COLS, B_COLS, W_C, W_C, W_C)],
        compiler_params=_cparams(("parallel",)),
        name="in_proj",
    )(x, gain, wa, wb, wc)


QUAD = 4 * N_A


def _block_ones(n, dtype):
    r = lax.broadcasted_iota(jnp.int32, (n, n), 0) // N_A
    c = lax.broadcasted_iota(jnp.int32, (n, n), 1) // N_A
    return (r == c).astype(dtype)


def _head_sum(x, ones):
    hi = x.astype(BF16)
    lo = (x - hi.astype(F32)).astype(BF16)
    return (jnp.dot(hi, ones, preferred_element_type=F32)
            + jnp.dot(lo, ones, preferred_element_type=F32))


def _rwkv_kernel(pa_ref, shift0_ref, s0_ref, mu_ref, w0_ref, w2_ref, a0_ref, a2_ref, g2_ref,
                 kk_ref, ka_ref, rk_ref, lnw_ref, lnb_ref,
                 out_ref, sT_ref, shiftT_ref,
                 s_scr, carry_scr, r_scr, y_scr, k_scr, v_scr, d_scr, al_scr, be_scr, g_scr, bonus_scr):
    nb, lt = pa_ref.shape[0], pa_ref.shape[1]
    nquad = W_A // QUAD
    ti = pl.program_id(1)

    @pl.when(ti == 0)
    def _():
        s_scr[...] = s0_ref[...].reshape(nb * nquad, N_A, QUAD)
        carry_scr[...] = shift0_ref[...]

    ones = _block_ones(W_A, BF16)
    row = lax.broadcasted_iota(jnp.int32, (lt, A_COLS), 0)
    for b in range(nb):
        x = pa_ref[b]
        prev = jnp.where(row == 0, carry_scr[b], pltpu.roll(x, 1, axis=0))
        carry_scr[b] = x[lt - 1:lt, :]
        xm = x + (prev - x) * mu_ref[...]
        r = xm[:, 0:W_A]
        k = xm[:, W_A:2 * W_A]
        v = xm[:, 2 * W_A:3 * W_A]
        lr = xm[:, 3 * W_A:3 * W_A + R_W + R_A]
        gd = xm[:, 3 * W_A + R_W + R_A:A_COLS]
        w = -jax.nn.softplus(-(w0_ref[...] + _fdot(jnp.tanh(lr), w2_ref[...]))) - 0.5
        a = jax.nn.sigmoid(a0_ref[...] + _fdot(lr, a2_ref[...]))
        kk = k * kk_ref[...]
        kk = kk * lax.rsqrt(jnp.maximum(_head_sum(kk * kk, ones), 1e-24))
        k = k * (1.0 + (a - 1.0) * ka_ref[...])
        g_scr[b] = _bdot(jax.nn.sigmoid(gd), g2_ref[...])
        bonus_scr[b] = _head_sum(r * k * rk_ref[...], ones) * v
        for q in range(nquad):
            i, cols = b * nquad + q, slice(q * QUAD, (q + 1) * QUAD)
            r_scr[i, 0:8, :] = jnp.zeros((8, QUAD), F32)
            r_scr[i, 8:lt + 8, :] = r[:, cols]
            k_scr[i] = k[:, cols]
            v_scr[i] = v[:, cols]
            d_scr[i] = jnp.exp(-jnp.exp(w))[:, cols]
            al_scr[i] = -kk[:, cols]
            be_scr[i] = (kk * a)[:, cols]

    ones_q = _block_ones(QUAD, BF16)
    lane = lax.broadcasted_iota(jnp.int32, (N_A, QUAD), 1)
    sub = lax.broadcasted_iota(jnp.int32, (N_A, QUAD), 0)
    diag = (lane % N_A) == sub

    def to_rows8(z):
        return jnp.sum(jnp.where(diag, z, 0.0).reshape(N_A // 8, 8, QUAD), axis=0)

    def put_y(i, rows, y8):
        y_scr[2 * i, rows, :] = y8[:, 0:LANES]
        y_scr[2 * i + 1, rows, :] = y8[:, LANES:QUAD]

    def step(t, carry):
        for g in range(0, nb * nquad, RWKV_GROUP):
            group = range(g, g + RWKV_GROUP)
            states, parts = [], []
            for i in group:
                s = s_scr[i]
                sb = s.astype(BF16)
                states.append(s)
                parts += [sb * al_scr[i, pl.ds(t, 1), :].astype(BF16),
                          sb * r_scr[i, pl.ds(t + 7, 1), :].astype(BF16),
                          jnp.where(diag, v_scr[i, pl.ds(t, 1), :], 0.0).astype(BF16)]
            o = jnp.dot(jnp.concatenate(parts, axis=0), ones_q, preferred_element_type=F32)
            for j, i in enumerate(group):
                sa, y_prev, vcol = (o[(3 * j + n) * N_A:(3 * j + n + 1) * N_A] for n in range(3))
                s_scr[i] = (states[j] * d_scr[i, pl.ds(t, 1), :] + sa * be_scr[i, pl.ds(t, 1), :]
                            + vcol * k_scr[i, pl.ds(t, 1), :])
                put_y(i, pl.ds(pl.multiple_of(t * 8, 8), 8), to_rows8(y_prev))
        return carry

    lax.fori_loop(0, lt, step, 0, unroll=min(RWKV_UNROLL, lt))

    for b in range(nb):
        for q in range(nquad):
            i = b * nquad + q
            o = jnp.dot((s_scr[i] * r_scr[i, lt + 7:lt + 8, :]).astype(BF16), ones_q,
                        preferred_element_type=F32)
            put_y(i, slice(lt * 8, (lt + 1) * 8), to_rows8(o))

        def half_y(n):
            tot = y_scr[n, pl.ds(8, lt, stride=8), :]
            for j in range(1, 8):
                tot = tot + y_scr[n, pl.ds(8 + j, lt, stride=8), :]
            return tot

        y = jnp.concatenate([half_y(2 * b * nquad + n) for n in range(2 * nquad)], axis=1)
        yc = y - _head_sum(y, ones) * (1.0 / N_A)
        yn = yc * lax.rsqrt(_head_sum(yc * yc, ones) * (1.0 / N_A) + GN_EPS_A)
        out_ref[b] = (yn * lnw_ref[...] + lnb_ref[...] + bonus_scr[b]) * g_scr[b]

    @pl.when(ti == pl.num_programs(1) - 1)
    def _():
        sT_ref[...] = s_scr[...].reshape(nb, nquad, N_A, QUAD)
        shiftT_ref[...] = carry_scr[...]


def _pack_wkv(s):
    n = s.shape[0]
    return s.reshape(n, H_A // 4, 4, N_A, N_A).transpose(0, 1, 3, 2, 4).reshape(n, H_A // 4, N_A, QUAD)


def _unpack_wkv(s):
    n = s.shape[0]
    return s.reshape(n, H_A // 4, N_A, 4, N_A).transpose(0, 1, 3, 2, 4).reshape(n, H_A, N_A, N_A)


def rwkv(pa, shift0, wkv0, P, layer):
    nseq, L, _ = pa.shape
    nb = RWKV_SEQS
    lt = min(RWKV_TILE, L)
    nquad = W_A // QUAD
    seq3 = lambda c: pl.BlockSpec((nb, lt, c), lambda s, t: (s, t, 0))
    vec = lambda name: _layer(P[name].shape[1:], layer)
    state = pl.BlockSpec((nb, nquad, N_A, QUAD), lambda s, t: (s, 0, 0, 0))
    slab = lambda rows: pltpu.VMEM((nb * nquad, rows, QUAD), F32)
    tile = lambda: pltpu.VMEM((nb, lt, W_A), F32)
    out, sT, shiftT = pl.pallas_call(
        _rwkv_kernel,
        grid=(nseq // nb, L // lt),
        in_specs=[seq3(A_COLS), pl.BlockSpec((nb, 1, A_COLS), lambda s, t: (s, 0, 0)), state,
                  vec('mu_a'), vec('w0_a'), vec('w2_a'), vec('a0_a'), vec('a2_a'), vec('g2_a'),
                  vec('k_k'), vec('k_a'), vec('r_k'), vec('lnx_w'), vec('lnx_b')],
        out_specs=[seq3(W_A), state, pl.BlockSpec((nb, 1, A_COLS), lambda s, t: (s, 0, 0))],
        out_shape=[jax.ShapeDtypeStruct((nseq, L, W_A), F32),
                   jax.ShapeDtypeStruct((nseq, nquad, N_A, QUAD), F32),
                   jax.ShapeDtypeStruct((nseq, 1, A_COLS), F32)],
        scratch_shapes=[pltpu.VMEM((nb * nquad, N_A, QUAD), F32), pltpu.VMEM((nb, 1, A_COLS), F32),
                        slab(lt + 8), pltpu.VMEM((2 * nb * nquad, (lt + 1) * 8, LANES), F32)]
                       + [slab(lt) for _ in range(5)] + [tile(), tile()],
        compiler_params=_cparams(("parallel", "arbitrary")),
        name="rwkv",
    )(pa, shift0[:, None, :], _pack_wkv(wkv0), P['mu_a'], P['w0_a'], P['w2_a'], P['a0_a'],
      P['a2_a'], P['g2_a'], P['k_k'], P['k_a'], P['r_k'], P['lnx_w'], P['lnx_b'])
    return out, _unpack_wkv(sT), shiftT[:, 0, :]


def _swap_halves(x):
    lane = lax.broadcasted_iota(jnp.int32, x.shape, 1)
    first = (lane % DK_B) < (DK_B // 2)
    return jnp.where(first, pltpu.roll(x, QK_B - DK_B // 2, axis=1), pltpu.roll(x, DK_B // 2, axis=1))


def _retention_kernel(pb_ref, r0_ref, cos_ref, sin_ref, dmat_ref, rowdec_ref, keydec_ref, chdec_ref,
                      out_ref, rT_ref, r_scr):
    ci = pl.program_id(1)

    @pl.when(ci == 0)
    def _():
        r_scr[...] = r0_ref[...]

    x = pb_ref[...]
    q = x[:, 0:QK_B]
    k = x[:, QK_B:2 * QK_B]
    q = q * cos_ref[...] + _swap_halves(q) * sin_ref[...]
    k = (k * cos_ref[...] + _swap_halves(k) * sin_ref[...]) * (DK_B ** -0.5)
    for h in range(H_B):
        qh = q[:, h * DK_B:(h + 1) * DK_B]
        kh = k[:, h * DK_B:(h + 1) * DK_B]
        vh = x[:, 2 * QK_B + h * DV_B:2 * QK_B + (h + 1) * DV_B]
        gh = x[:, 2 * QK_B + W_B + h * DV_B:2 * QK_B + W_B + (h + 1) * DV_B]
        rs = r_scr[h]
        s = _bdot_nt(qh, kh) * dmat_ref[h]
        o = _bdot(s, vh) + _bdot(qh, rs) * rowdec_ref[h]
        r_scr[h] = rs * chdec_ref[h] + _bdot_tn(kh * keydec_ref[h], vh)
        o = o * lax.rsqrt(jnp.mean(o * o, axis=-1, keepdims=True) + NORM_EPS)
        out_ref[:, h * DV_B:(h + 1) * DV_B] = o * jax.nn.silu(gh)

    @pl.when(ci == pl.num_programs(1) - 1)
    def _():
        rT_ref[...] = r_scr[...]


def _retention_tables(L, lc, pos0):
    lg = jnp.log1p(-jnp.exp2(-5.0 - jnp.arange(H_B, dtype=F32)))
    idx = jnp.arange(lc, dtype=F32)
    diff = idx[:, None] - idx[None, :]
    dmat = jnp.where(diff >= 0, jnp.exp(jnp.maximum(diff, 0.0)[None] * lg[:, None, None]), 0.0)
    rowdec = jnp.exp((idx[None, :] + 1.0) * lg[:, None])[..., None]
    keydec = jnp.exp((lc - 1.0 - idx)[None, :] * lg[:, None])[..., None]
    chdec = jnp.exp(lc * lg)[:, None, None]
    half = DK_B // 2
    inv = ROPE_BASE ** (-jnp.arange(half, dtype=F32) / half)
    ang = (pos0 + jnp.arange(L)).astype(F32)[:, None] * inv[None, :]
    cos, sin = jnp.cos(ang), jnp.sin(ang)
    cos_t = jnp.tile(jnp.concatenate([cos, cos], axis=1), (1, H_B))
    sin_t = jnp.tile(jnp.concatenate([-sin, sin], axis=1), (1, H_B))
    return cos_t, sin_t, dmat, rowdec, keydec, chdec


def retention(pb, ret0, pos0):
    nseq, L, _ = pb.shape
    lc = min(RET_CHUNK, L)
    cos_t, sin_t, dmat, rowdec, keydec, chdec = _retention_tables(L, lc, pos0)
    state = pl.BlockSpec((None, H_B, DK_B, DV_B), lambda s, c: (s, 0, 0, 0))
    tab = pl.BlockSpec((lc, QK_B), lambda s, c: (c, 0))
    out, rT = pl.pallas_call(
        _retention_kernel,
        grid=(nseq, L // lc),
        in_specs=[pl.BlockSpec((None, lc, B_COLS), lambda s, c: (s, c, 0)), state, tab, tab,
                  _full((H_B, lc, lc)), _full((H_B, lc, 1)), _full((H_B, lc, 1)), _full((H_B, 1, 1))],
        out_specs=[pl.BlockSpec((None, lc, W_B), lambda s, c: (s, c, 0)), state],
        out_shape=[jax.ShapeDtypeStruct((nseq, L, W_B), F32),
                   jax.ShapeDtypeStruct((nseq, H_B, DK_B, DV_B), F32)],
        scratch_shapes=[pltpu.VMEM((H_B, DK_B, DV_B), F32)],
        compiler_params=_cparams(("parallel", "arbitrary")),
        name="retention",
    )(pb, ret0, cos_t, sin_t, dmat, rowdec, keydec, chdec)
    return out, rT


def _lam(lq1, lk1, lq2, lk2, lam_init):
    return (jnp.exp(jnp.sum(lq1 * lk1, axis=-1, keepdims=True))
            - jnp.exp(jnp.sum(lq2 * lk2, axis=-1, keepdims=True)) + lam_init)


def _two_map_queries(q):
    lane = lax.broadcasted_iota(jnp.int32, q.shape, 1)
    q = q * (DH_C ** -0.5)
    return jnp.concatenate([jnp.where(lane < DH_C, q, 0.0), jnp.where(lane < DH_C, 0.0, q)],
                           axis=0).astype(BF16)


def _softmax_update(s, pv, m_ref, l_ref, acc_ref, rows=slice(None)):
    m_old = m_ref[rows]
    m_new = jnp.maximum(m_old, jnp.max(s, axis=-1, keepdims=True))
    alpha = jnp.exp(m_old - m_new)
    p = jnp.exp(s - m_new[:, 0:1])
    l_ref[rows] = alpha * l_ref[rows] + jnp.sum(p, axis=-1, keepdims=True)
    acc_ref[rows] = alpha * acc_ref[rows] + pv(p)
    m_ref[rows] = m_new


def _attn_prompt_kernel(lam_init, has_prefix, qi_ref, kj_ref, q_ref, k_ref, v_ref, *rest):
    if has_prefix:
        kp_ref, vp_ref, *rest = rest
    lq1, lk1, lq2, lk2, subln_ref, out_ref, qb_scr, m_scr, l_scr, acc_scr = rest
    step = pl.program_id(2)
    qi, kj = qi_ref[step], kj_ref[step]
    bq = q_ref.shape[0]

    def process(k, v, mask):
        k, v = k.astype(BF16), v.astype(BF16)
        rc = min(ATTN_ROWS, bq)
        for r0 in range(0, 2 * bq, rc):
            s = _bdot_nt(qb_scr[r0:r0 + rc], k)
            if mask:
                r = lax.broadcasted_iota(jnp.int32, s.shape, 0) + (r0 % bq)
                c = lax.broadcasted_iota(jnp.int32, s.shape, 1)
                s = jnp.where(c <= r, s, NEG_INF)
            _softmax_update(s, lambda p: _bdot(p, v), m_scr, l_scr, acc_scr, slice(r0, r0 + rc))

    @pl.when(kj == 0)
    def _():
        qb_scr[...] = _two_map_queries(q_ref[...])
        m_scr[...] = jnp.full(m_scr.shape, NEG_INF, F32)
        l_scr[...] = jnp.zeros(l_scr.shape, F32)
        acc_scr[...] = jnp.zeros(acc_scr.shape, F32)
        if has_prefix:
            process(kp_ref[...], vp_ref[...], False)

    @pl.when(kj < qi)
    def _():
        process(k_ref[...], v_ref[...], False)

    @pl.when(kj == qi)
    def _():
        process(k_ref[...], v_ref[...], True)
        lam = _lam(lq1[...], lk1[...], lq2[...], lk2[...], lam_init)
        o = acc_scr[0:bq] / l_scr[0:bq] - lam * (acc_scr[bq:2 * bq] / l_scr[bq:2 * bq])
        out_ref[...] = _rms(o, subln_ref[...], SUBLN_EPS) * (1.0 - lam_init)


def attn_prompt(q, k, v, prefix, P, layer, lam_init):
    nseq, L, _ = q.shape
    blk = min(ATTN_BLOCK, L)
    nblk = L // blk
    pairs = [(i, j) for i in range(nblk) for j in range(i + 1)]
    qi_tab = jnp.array([p[0] for p in pairs], jnp.int32)
    kj_tab = jnp.array([p[1] for p in pairs], jnp.int32)
    qspec = pl.BlockSpec((None, blk, DV_C), lambda b, h, s, qi, kj: (b, qi[s], h))
    kspec = pl.BlockSpec((None, blk, DV_C), lambda b, h, s, qi, kj: (b, kj[s], h))
    args, specs = [q, k, v], [qspec, kspec, kspec]
    if prefix is not None:
        npre = prefix[0].shape[1]
        pspec = pl.BlockSpec((None, npre, DV_C), lambda b, h, s, qi, kj: (b, 0, h))
        args += list(prefix)
        specs += [pspec, pspec]
    lspec = lambda c: pl.BlockSpec((None, 1, c), lambda b, h, s, qi, kj: (layer, 0, 0))
    for name in ('lam_q1', 'lam_k1', 'lam_q2', 'lam_k2'):
        args.append(P[name])
        specs.append(lspec(DH_C))
    args.append(P['subln'])
    specs.append(lspec(DV_C))
    wide = lambda dt: pltpu.VMEM((2 * blk, DV_C), dt)
    grid_spec = pltpu.PrefetchScalarGridSpec(
        num_scalar_prefetch=2,
        grid=(nseq, H_C, len(pairs)),
        in_specs=specs,
        out_specs=qspec,
        scratch_shapes=[wide(BF16), wide(F32), wide(F32), wide(F32)],
    )
    return pl.pallas_call(
        lambda *refs: _attn_prompt_kernel(lam_init, prefix is not None, *refs),
        grid_spec=grid_spec,
        out_shape=jax.ShapeDtypeStruct((nseq, L, W_C), F32),
        compiler_params=_cparams(("parallel", "parallel", "arbitrary")),
        name="attn_prompt",
    )(qi_tab, kj_tab, *args)


def _attn_sample_kernel(lam_init, npp, pt_ref, q_ref, kn_ref, vn_ref, lq1, lk1, lq2, lk2, subln_ref,
                        *rest):
    k_refs, v_refs = rest[:npp], rest[npp:2 * npp]
    out_ref, qb_scr, m_scr, l_scr, acc_scr = rest[2 * npp:]
    j = pl.program_id(1)
    nq = q_ref.shape[0]
    hr = 2 * nq
    head = lambda h: slice(h * DV_C, (h + 1) * DV_C)
    rows = lambda h: slice(h * hr, (h + 1) * hr)

    @pl.when(j == 0)
    def _():
        for h in range(H_C):
            qb_scr[rows(h), :] = _two_map_queries(q_ref[:, head(h)])
        m_scr[...] = jnp.full(m_scr.shape, NEG_INF, F32)
        l_scr[...] = jnp.zeros(l_scr.shape, F32)
        acc_scr[...] = jnp.zeros(acc_scr.shape, F32)

    of_head = lambda ref, h: ref[pl.ds(h, PAGE_SIZE, stride=H_C), :]
    s = jnp.concatenate(
        [jnp.concatenate([_bdot_nt(qb_scr[rows(h), :], of_head(k_refs[i], h)) for i in range(npp)], axis=1)
         for h in range(H_C)], axis=0)

    def pv_pages(p):
        outs = []
        for h in range(H_C):
            tot = None
            for i in range(npp):
                t = _bdot(p[rows(h), i * PAGE_SIZE:(i + 1) * PAGE_SIZE], of_head(v_refs[i], h))
                tot = t if tot is None else tot + t
            outs.append(tot)
        return jnp.concatenate(outs, axis=0)

    _softmax_update(s, pv_pages, m_scr, l_scr, acc_scr)

    @pl.when(j == pl.num_programs(1) - 1)
    def _():
        sn = jnp.concatenate([_bdot_nt(qb_scr[rows(h), :], kn_ref[:, head(h)]) for h in range(H_C)], axis=0)
        r = lax.broadcasted_iota(jnp.int32, sn.shape, 0)
        c = lax.broadcasted_iota(jnp.int32, sn.shape, 1)
        sn = jnp.where(c <= r % nq, sn, NEG_INF)
        _softmax_update(
            sn, lambda p: jnp.concatenate([_bdot(p[rows(h), :], vn_ref[:, head(h)]) for h in range(H_C)], axis=0),
            m_scr, l_scr, acc_scr)
        lam = _lam(lq1[...], lk1[...], lq2[...], lk2[...], lam_init)
        o_all = acc_scr[...] / l_scr[...]
        for h in range(H_C):
            o = o_all[h * hr:h * hr + nq] - lam * o_all[h * hr + nq:(h + 1) * hr]
            out_ref[:, head(h)] = _rms(o, subln_ref[...], SUBLN_EPS) * (1.0 - lam_init)


def attn_sample(q, k, v, cache_k, cache_v, page_table, P, layer, lam_init):
    nseq, nq, _ = q.shape
    n_pages = page_table.shape[1]
    npp = PAGES_PER_STEP
    seq = pl.BlockSpec((None, nq, W_C), lambda b, j, pt: (b, 0, 0))

    n_pool = cache_k.shape[1]
    cache_k = cache_k.reshape(DEPTH, n_pool, PAGE_SIZE * H_C, DV_C)
    cache_v = cache_v.reshape(DEPTH, n_pool, PAGE_SIZE * H_C, DV_C)

    def page(i):
        return pl.BlockSpec((None, None, PAGE_SIZE * H_C, DV_C),
                            lambda b, j, pt: (layer, pt[b * n_pages + j * npp + i], 0, 0))

    lspec = lambda c: pl.BlockSpec((None, 1, c), lambda b, j, pt: (layer, 0, 0))
    rows = H_C * 2 * nq
    grid_spec = pltpu.PrefetchScalarGridSpec(
        num_scalar_prefetch=1,
        grid=(nseq, n_pages // npp),
        in_specs=[seq, seq, seq, lspec(DH_C), lspec(DH_C), lspec(DH_C), lspec(DH_C), lspec(DV_C)]
                 + [page(i) for i in range(npp)] + [page(i) for i in range(npp)],
        out_specs=seq,
        scratch_shapes=[pltpu.VMEM((rows, DV_C), BF16), pltpu.VMEM((rows, DV_C), F32),
                        pltpu.VMEM((rows, DV_C), F32), pltpu.VMEM((rows, DV_C), F32)],
    )
    return pl.pallas_call(
        lambda *refs: _attn_sample_kernel(lam_init, npp, *refs),
        grid_spec=grid_spec,
        out_shape=jax.ShapeDtypeStruct((nseq, nq, W_C), F32),
        compiler_params=_cparams(("parallel", "arbitrary")),
        name="attn_sample",
    )(page_table.reshape(-1), q, k, v, P['lam_q1'], P['lam_k1'], P['lam_q2'], P['lam_k2'], P['subln'],
      *([cache_k] * npp), *([cache_v] * npp))


def _merge_kernel(x_ref, oa_ref, ob_ref, oc_ref, g_ref, wg_ref, wpa_ref, wpb_ref, wpc_ref, wo_ref,
                  out_ref):
    x = x_ref[...]
    h = _rms(x, g_ref[...], NORM_EPS).astype(BF16)
    merged = None
    for i, (o_ref, w_ref) in enumerate(((oa_ref, wpa_ref), (ob_ref, wpb_ref), (oc_ref, wpc_ref))):
        gate = jax.nn.sigmoid(jnp.dot(h, wg_ref[:, i * D_MODEL:(i + 1) * D_MODEL],
                                      preferred_element_type=F32))
        term = gate * jnp.dot(o_ref[...].astype(BF16), w_ref[...], preferred_element_type=F32)
        merged = term if merged is None else merged + term
    out_ref[...] = x + jnp.dot(merged.astype(BF16), wo_ref[...], preferred_element_type=F32)


def merge(x, oa, ob, oc, P, layer):
    n = x.shape[0]
    tm = min(ROW_TILE, n)
    row = lambda c: pl.BlockSpec((tm, c), lambda i: (i, 0))
    return pl.pallas_call(
        _merge_kernel,
        grid=(n // tm,),
        in_specs=[row(D_MODEL), row(W_A), row(W_B), row(W_C), _layer((1, D_MODEL), layer),
                  _layer((D_MODEL, 3 * D_MODEL), layer), _layer((W_A, D_MODEL), layer),
                  _layer((W_B, D_MODEL), layer), _layer((W_C, D_MODEL), layer),
                  _layer((D_MODEL, D_MODEL), layer)],
        out_specs=row(D_MODEL),
        out_shape=jax.ShapeDtypeStruct((n, D_MODEL), F32),
        compiler_params=_cparams(("parallel",)),
        name="merge",
    )(x, oa, ob, oc, P['norm1'], P['w_gate'], P['w_pa'], P['w_pb'], P['w_pc'], P['w_o'])


def _ffn_kernel(x_ref, c0_ref, g_ref, win_ref, cw_ref, cb_ref, wout_ref, out_ref, cT_ref, carry_scr):
    sb, tl = x_ref.shape[0], x_ref.shape[1]
    ti = pl.program_id(1)

    @pl.when(ti == 0)
    def _():
        carry_scr[...] = c0_ref[...]

    x = x_ref[...].reshape(sb * tl, D_MODEL)
    h = _rms(x, g_ref[...], NORM_EPS).astype(BF16)
    u = jnp.dot(h, win_ref[:, 0:D_FF], preferred_element_type=F32)
    gate = jnp.dot(h, win_ref[:, D_FF:2 * D_FF], preferred_element_type=F32)
    t = lax.broadcasted_iota(jnp.int32, (sb, tl, D_FF), 1).reshape(sb * tl, D_FF)
    bc = lambda j: jnp.broadcast_to(carry_scr[:, j:j + 1, :], (sb, tl, D_FF)).reshape(sb * tl, D_FF)
    prev1 = jnp.where(t == 0, bc(1), pltpu.roll(u, 1, axis=0))
    prev2 = jnp.where(t == 0, bc(0), jnp.where(t == 1, bc(1), pltpu.roll(u, 2, axis=0)))
    u3 = u.reshape(sb, tl, D_FF)
    carry_scr[...] = u3[:, tl - 2:tl, :]
    uc = cb_ref[...] + prev2 * cw_ref[0:1, :]
    uc = uc + prev1 * cw_ref[1:2, :]
    uc = uc + u * cw_ref[2:3, :]
    act = jax.nn.gelu(uc) * gate
    y = x + jnp.dot(act.astype(BF16), wout_ref[...], preferred_element_type=F32)
    out_ref[...] = y.reshape(sb, tl, D_MODEL)

    @pl.when(ti == pl.num_programs(1) - 1)
    def _():
        cT_ref[...] = carry_scr[...]


def ffn(x, conv0, P, layer, sb, tl):
    nseq, L, _ = x.shape
    blk = pl.BlockSpec((sb, tl, D_MODEL), lambda s, t: (s, t, 0))
    cst = pl.BlockSpec((sb, CONV_W - 1, D_FF), lambda s, t: (s, 0, 0))
    return pl.pallas_call(
        _ffn_kernel,
        grid=(nseq // sb, L // tl),
        in_specs=[blk, cst, _layer((1, D_MODEL), layer), _layer((D_MODEL, 2 * D_FF), layer),
                  _layer((CONV_W, D_FF), layer), _layer((1, D_FF), layer), _layer((D_FF, D_MODEL), layer)],
        out_specs=[blk, cst],
        out_shape=[jax.ShapeDtypeStruct((nseq, L, D_MODEL), F32),
                   jax.ShapeDtypeStruct((nseq, CONV_W - 1, D_FF), F32)],
        scratch_shapes=[pltpu.VMEM((sb, CONV_W - 1, D_FF), F32)],
        compiler_params=_cparams(("parallel", "arbitrary")),
        name="ffn",
    )(x, conv0, P['norm2'], P['w_ffn_in'], P['conv_w'], P['conv_b'], P['w_ffn_out'])


def _norm_kernel(x_ref, g_ref, out_ref):
    out_ref[...] = _rms(x_ref[...], g_ref[...], NORM_EPS)


def final_norm(x, gain):
    n = x.shape[0]
    tm = min(ROW_TILE, n)
    row = pl.BlockSpec((tm, D_MODEL), lambda i: (i, 0))
    return pl.pallas_call(
        _norm_kernel,
        grid=(n // tm,),
        in_specs=[row, _full((1, D_MODEL))],
        out_specs=row,
        out_shape=jax.ShapeDtypeStruct((n, D_MODEL), F32),
        compiler_params=_cparams(("parallel",)),
        name="final_norm",
    )(x, gain)


def _prepare_params(W):
    P = {}
    w_in = W['w_in']
    P['w_a'] = w_in[:, :, 0:A_COLS].astype(BF16)
    P['w_b'] = w_in[:, :, A_COLS:A_COLS + B_COLS].astype(BF16)
    P['w_c'] = w_in[:, :, A_COLS + B_COLS:A_COLS + B_COLS + C_COLS].astype(BF16)
    P['w_gate'] = w_in[:, :, A_COLS + B_COLS + C_COLS:].astype(BF16)
    for name in ('w_pa', 'w_pb', 'w_pc', 'w_o', 'w_ffn_in', 'w_ffn_out'):
        P[name] = W[name].astype(BF16)
    for name in ('norm1', 'norm2', 'mu_a', 'w0_a', 'a0_a', 'k_k', 'k_a', 'lnx_w', 'lnx_b', 'conv_b',
                 'lam_q1', 'lam_k1', 'lam_q2', 'lam_k2', 'subln'):
        P[name] = W[name][:, None, :]
    P['r_k'] = W['r_k'].reshape(DEPTH, 1, W_A)
    P['w2_a'] = jnp.concatenate([W['w2_a'], jnp.zeros((DEPTH, R_A, W_A), F32)], axis=1)
    P['a2_a'] = jnp.concatenate([jnp.zeros((DEPTH, R_W, W_A), F32), W['a2_a']], axis=1)
    P['g2_a'] = W['g2_a']
    P['conv_w'] = W['conv_w']
    return P


def _layer_group(x, states, pos0, attn_fn, P, layer, ffn_blk):
    nseq, L, _ = x.shape
    wkv0, shift0, ret0, conv0 = states
    flat = lambda z: z.reshape(nseq * L, z.shape[-1])
    seq = lambda z: z.reshape(nseq, L, z.shape[-1])
    pa, pb, q, k, v = in_proj(flat(x), P['norm1'], P['w_a'], P['w_b'], P['w_c'], layer)
    out_a, wkv_t, shift_t = rwkv(seq(pa), shift0, wkv0, P, layer)
    out_b, ret_t = retention(seq(pb), ret0, pos0)
    lam_init = 0.8 - 0.6 * math.exp(-0.3 * layer)
    q, k, v = seq(q), seq(k), seq(v)
    out_c = attn_fn(q, k, v, lam_init)
    x = merge(flat(x), flat(out_a), flat(out_b), flat(out_c), P, layer)
    x, conv_t = ffn(seq(x), conv0, P, layer, *ffn_blk)
    return x, (k, v, wkv_t, shift_t, ret_t, conv_t)


def kernel(x_prompt, x_sample, cache_k, cache_v, page_table, state_wkv, state_shift, state_ret,
           state_conv, meta, norm1, w_in, mu_a, w0_a, w2_a, a0_a, a2_a, g2_a, k_k, k_a, r_k,
           lnx_w, lnx_b, lam_q1, lam_k1, lam_q2, lam_k2, subln, w_pa, w_pb, w_pc, w_o, norm2,
           w_ffn_in, conv_w, conv_b, w_ffn_out, norm_f):
    W = dict(norm1=norm1, w_in=w_in, mu_a=mu_a, w0_a=w0_a, w2_a=w2_a, a0_a=a0_a, a2_a=a2_a,
             g2_a=g2_a, k_k=k_k, k_a=k_a, r_k=r_k, lnx_w=lnx_w, lnx_b=lnx_b, lam_q1=lam_q1,
             lam_k1=lam_k1, lam_q2=lam_q2, lam_k2=lam_k2, subln=subln, w_pa=w_pa, w_pb=w_pb,
             w_pc=w_pc, w_o=w_o, norm2=norm2, w_ffn_in=w_ffn_in, conv_w=conv_w, conv_b=conv_b,
             w_ffn_out=w_ffn_out)
    P = _prepare_params(W)
    b, t_main, _ = x_prompt.shape
    bd, ds, _ = x_sample.shape
    past = page_table.shape[1] * PAGE_SIZE

    x_meta = jnp.broadcast_to(meta[None].astype(F32), (b, N_META, D_MODEL))
    x_main = x_prompt
    x_samp = x_sample
    zeros = lambda *s: jnp.zeros(s, F32)
    meta_states = (zeros(b, H_A, N_A, N_A), zeros(b, A_COLS), zeros(b, H_B, DK_B, DV_B),
                   zeros(b, CONV_W - 1, D_FF))
    outs_p = [[] for _ in range(6)]
    outs_s = [[] for _ in range(6)]
    for l in range(DEPTH):
        x_meta, st_meta = _layer_group(
            x_meta, meta_states, 0,
            lambda q, k, v, li: attn_prompt(q, k, v, None, P, l, li), P, l, (b, N_META))
        km, vm = st_meta[0], st_meta[1]
        x_main, st_main = _layer_group(
            x_main, st_meta[2:], N_META,
            lambda q, k, v, li: attn_prompt(q, k, v, (km, vm), P, l, li), P, l,
            (1, min(ROW_TILE, t_main)))
        x_samp, st_samp = _layer_group(
            x_samp, (state_wkv[l], state_shift[l], state_ret[l], state_conv[l]), past,
            lambda q, k, v, li: attn_sample(q, k, v, cache_k, cache_v, page_table, P, l, li), P, l,
            (bd, ds))
        heads = lambda z: z.reshape(z.shape[0], z.shape[1], H_C, DV_C)
        outs_p[0].append(jnp.concatenate([heads(km), heads(st_main[0])], axis=1))
        outs_p[1].append(jnp.concatenate([heads(vm), heads(st_main[1])], axis=1))
        for i in range(2, 6):
            outs_p[i].append(st_main[i])
        outs_s[0].append(heads(st_samp[0]))
        outs_s[1].append(heads(st_samp[1]))
        for i in range(2, 6):
            outs_s[i].append(st_samp[i])
    gain_f = norm_f[None, :]
    y_prompt = final_norm(x_main.reshape(b * t_main, D_MODEL), gain_f).reshape(b, t_main, D_MODEL)
    y_sample = final_norm(x_samp.reshape(bd * ds, D_MODEL), gain_f).reshape(bd, ds, D_MODEL)
    stack = lambda lst: jnp.stack(lst, 0)
    return (y_prompt, y_sample, *[stack(o) for o in outs_p], *[stack(o) for o in outs_s])
```

```python
import math

import jax
import jax.numpy as jnp
from jax import lax
from jax.experimental import pallas as pl
from jax.experimental.pallas import tpu as pltpu

F32 = jnp.float32
BF16 = jnp.bfloat16

D_MODEL = 1024
DEPTH = 4
N_META = 16
H_A, N_A = 8, 64
W_A = H_A * N_A
R_W, R_A, R_G = 64, 64, 128
H_B, DK_B, DV_B = 4, 64, 128
QK_B, W_B = H_B * DK_B, H_B * DV_B
ROPE_BASE = 10000.0
H_C, DH_C = 4, 64
DV_C = 2 * DH_C
W_C = H_C * DV_C
D_FF = 2816
CONV_W = 3
A_COLS = 3 * W_A + R_W + R_A + R_G
B_COLS = 2 * QK_B + 2 * W_B
C_COLS = 3 * W_C
NORM_EPS = 1e-6
GN_EPS_A = 64e-5
SUBLN_EPS = 1e-5
NEG_INF = -1e30
PAGE_SIZE = 128

LANES = 128
VMEM_LIMIT = 52 * 1024 * 1024
ROW_TILE = 512
RWKV_SEQS = 4
RWKV_TILE = 128
RWKV_GROUP = 2
RWKV_UNROLL = 8
RET_CHUNK = 128
ATTN_BLOCK = 1024
ATTN_ROWS = 256
PAGES_PER_STEP = 16

HIGHEST = lax.Precision.HIGHEST


def _cparams(sem):
    return pltpu.CompilerParams(dimension_semantics=sem, vmem_limit_bytes=VMEM_LIMIT)


def _rms(x, gain, eps):
    return x * lax.rsqrt(jnp.mean(x * x, axis=-1, keepdims=True) + eps) * gain


def _bdot(a, b):
    return jnp.dot(a.astype(BF16), b.astype(BF16), preferred_element_type=F32)


def _bdot_nt(a, b):
    return lax.dot_general(a.astype(BF16), b.astype(BF16), (((1,), (1,)), ((), ())),
                           preferred_element_type=F32)


def _bdot_tn(a, b):
    return lax.dot_general(a.astype(BF16), b.astype(BF16), (((0,), (0,)), ((), ())),
                           preferred_element_type=F32)


def _fdot(a, b):
    return jnp.dot(a, b, precision=HIGHEST, preferred_element_type=F32)


def _full(shape):
    return pl.BlockSpec(shape, lambda *_: (0,) * len(shape))


def _layer(shape, layer):
    return pl.BlockSpec((None,) + shape, lambda *_: (layer,) + (0,) * len(shape))


def _in_proj_kernel(x_ref, g_ref, wa_ref, wb_ref, wc_ref, pa_ref, pb_ref, q_ref, k_ref, v_ref):
    h = _rms(x_ref[...], g_ref[...], NORM_EPS).astype(BF16)
    pa_ref[...] = jnp.dot(h, wa_ref[...], preferred_element_type=F32)
    pb_ref[...] = jnp.dot(h, wb_ref[...], preferred_element_type=F32)
    q_ref[...] = jnp.dot(h, wc_ref[:, 0:W_C], preferred_element_type=F32)
    k_ref[...] = jnp.dot(h, wc_ref[:, W_C:2 * W_C], preferred_element_type=F32)
    v_ref[...] = jnp.dot(h, wc_ref[:, 2 * W_C:3 * W_C], preferred_element_type=F32)


def in_proj(x, gain, wa, wb, wc, layer):
    n = x.shape[0]
    tm = min(ROW_TILE, n)
    row = lambda c: pl.BlockSpec((tm, c), lambda i: (i, 0))
    return pl.pallas_call(
        _in_proj_kernel,
        grid=(n // tm,),
        in_specs=[row(D_MODEL), _layer((1, D_MODEL), layer), _layer((D_MODEL, A_COLS), layer),
                  _layer((D_MODEL, B_COLS), layer), _layer((D_MODEL, C_COLS), layer)],
        out_specs=[row(A_COLS), row(B_COLS), row(W_C), row(W_C), row(W_C)],
        out_shape=[jax.ShapeDtypeStruct((n, c), F32) for c in (A_COLS, B_COLS, W_C, W_C, W_C)],
        compiler_params=_cparams(("parallel",)),
        name="in_proj",
    )(x, gain, wa, wb, wc)


QUAD = 4 * N_A


def _block_ones(n, dtype):
    r = lax.broadcasted_iota(jnp.int32, (n, n), 0) // N_A
    c = lax.broadcasted_iota(jnp.int32, (n, n), 1) // N_A
    return (r == c).astype(dtype)


def _head_sum(x, ones):
    hi = x.astype(BF16)
    lo = (x - hi.astype(F32)).astype(BF16)
    return (jnp.dot(hi, ones, preferred_element_type=F32)
            + jnp.dot(lo, ones, preferred_element_type=F32))


def _rwkv_kernel(pa_ref, shift0_ref, s0_ref, mu_ref, w0_ref, w2_ref, a0_ref, a2_ref, g2_ref,
                 kk_ref, ka_ref, rk_ref, lnw_ref, lnb_ref,
                 out_ref, sT_ref, shiftT_ref,
                 s_scr, carry_scr, r_scr, y_scr, k_scr, v_scr, d_scr, al_scr, be_scr, g_scr, bonus_scr):
    nb, lt = pa_ref.shape[0], pa_ref.shape[1]
    nquad = W_A // QUAD
    ti = pl.program_id(1)

    @pl.when(ti == 0)
    def _():
        s_scr[...] = s0_ref[...].reshape(nb * nquad, N_A, QUAD)
        carry_scr[...] = shift0_ref[...]

    ones = _block_ones(W_A, BF16)
    row = lax.broadcasted_iota(jnp.int32, (lt, A_COLS), 0)
    for b in range(nb):
        x = pa_ref[b]
        prev = jnp.where(row == 0, carry_scr[b], pltpu.roll(x, 1, axis=0))
        carry_scr[b] = x[lt - 1:lt, :]
        xm = x + (prev - x) * mu_ref[...]
        r = xm[:, 0:W_A]
        k = xm[:, W_A:2 * W_A]
        v = xm[:, 2 * W_A:3 * W_A]
        lr = xm[:, 3 * W_A:3 * W_A + R_W + R_A]
        gd = xm[:, 3 * W_A + R_W + R_A:A_COLS]
        w = -jax.nn.softplus(-(w0_ref[...] + _fdot(jnp.tanh(lr), w2_ref[...]))) - 0.5
        a = jax.nn.sigmoid(a0_ref[...] + _fdot(lr, a2_ref[...]))
        kk = k * kk_ref[...]
        kk = kk * lax.rsqrt(jnp.maximum(_head_sum(kk * kk, ones), 1e-24))
        k = k * (1.0 + (a - 1.0) * ka_ref[...])
        g_scr[b] = _bdot(jax.nn.sigmoid(gd), g2_ref[...])
        bonus_scr[b] = _head_sum(r * k * rk_ref[...], ones) * v
        for q in range(nquad):
            i, cols = b * nquad + q, slice(q * QUAD, (q + 1) * QUAD)
            r_scr[i, 0:8, :] = jnp.zeros((8, QUAD), F32)
            r_scr[i, 8:lt + 8, :] = r[:, cols]
            k_scr[i] = k[:, cols]
            v_scr[i] = v[:, cols]
            d_scr[i] = jnp.exp(-jnp.exp(w))[:, cols]
            al_scr[i] = -kk[:, cols]
            be_scr[i] = (kk * a)[:, cols]

    ones_q = _block_ones(QUAD, BF16)
    lane = lax.broadcasted_iota(jnp.int32, (N_A, QUAD), 1)
    sub = lax.broadcasted_iota(jnp.int32, (N_A, QUAD), 0)
    diag = (lane % N_A) == sub

    def to_rows8(z):
        return jnp.sum(jnp.where(diag, z, 0.0).reshape(N_A // 8, 8, QUAD), axis=0)

    def put_y(i, rows, y8):
        y_scr[2 * i, rows, :] = y8[:, 0:LANES]
        y_scr[2 * i + 1, rows, :] = y8[:, LANES:QUAD]

    def step(t, carry):
        for g in range(0, nb * nquad, RWKV_GROUP):
            group = range(g, g + RWKV_GROUP)
            states, parts = [], []
            for i in group:
                s = s_scr[i]
                sb = s.astype(BF16)
                states.append(s)
                parts += [sb * al_scr[i, pl.ds(t, 1), :].astype(BF16),
                          sb * r_scr[i, pl.ds(t + 7, 1), :].astype(BF16),
                          jnp.where(diag, v_scr[i, pl.ds(t, 1), :], 0.0).astype(BF16)]
            o = jnp.dot(jnp.concatenate(parts, axis=0), ones_q, preferred_element_type=F32)
            for j, i in enumerate(group):
                sa, y_prev, vcol = (o[(3 * j + n) * N_A:(3 * j + n + 1) * N_A] for n in range(3))
                s_scr[i] = (states[j] * d_scr[i, pl.ds(t, 1), :] + sa * be_scr[i, pl.ds(t, 1), :]
                            + vcol * k_scr[i, pl.ds(t, 1), :])
                put_y(i, pl.ds(pl.multiple_of(t * 8, 8), 8), to_rows8(y_prev))
        return carry

    lax.fori_loop(0, lt, step, 0, unroll=min(RWKV_UNROLL, lt))

    for b in range(nb):
        for q in range(nquad):
            i = b * nquad + q
            o = jnp.dot((s_scr[i] * r_scr[i, lt + 7:lt + 8, :]).astype(BF16), ones_q,
                        preferred_element_type=F32)
            put_y(i, slice(lt * 8, (lt + 1) * 8), to_rows8(o))

        def half_y(n):
            tot = y_scr[n, pl.ds(8, lt, stride=8), :]
            for j in range(1, 8):
                tot = tot + y_scr[n, pl.ds(8 + j, lt, stride=8), :]
            return tot

        y = jnp.concatenate([half_y(2 * b * nquad + n) for n in range(2 * nquad)], axis=1)
        yc = y - _head_sum(y, ones) * (1.0 / N_A)
        yn = yc * lax.rsqrt(_head_sum(yc * yc, ones) * (1.0 / N_A) + GN_EPS_A)
        out_ref[b] = (yn * lnw_ref[...] + lnb_ref[...] + bonus_scr[b]) * g_scr[b]

    @pl.when(ti == pl.num_programs(1) - 1)
    def _():
        sT_ref[...] = s_scr[...].reshape(nb, nquad, N_A, QUAD)
        shiftT_ref[...] = carry_scr[...]


def _pack_wkv(s):
    n = s.shape[0]
    return s.reshape(n, H_A // 4, 4, N_A, N_A).transpose(0, 1, 3, 2, 4).reshape(n, H_A // 4, N_A, QUAD)


def _unpack_wkv(s):
    n = s.shape[0]
    return s.reshape(n, H_A // 4, N_A, 4, N_A).transpose(0, 1, 3, 2, 4).reshape(n, H_A, N_A, N_A)


def rwkv(pa, shift0, wkv0, P, layer):
    nseq, L, _ = pa.shape
    nb = RWKV_SEQS
    lt = min(RWKV_TILE, L)
    nquad = W_A // QUAD
    seq3 = lambda c: pl.BlockSpec((nb, lt, c), lambda s, t: (s, t, 0))
    vec = lambda name: _layer(P[name].shape[1:], layer)
    state = pl.BlockSpec((nb, nquad, N_A, QUAD), lambda s, t: (s, 0, 0, 0))
    slab = lambda rows: pltpu.VMEM((nb * nquad, rows, QUAD), F32)
    tile = lambda: pltpu.VMEM((nb, lt, W_A), F32)
    out, sT, shiftT = pl.pallas_call(
        _rwkv_kernel,
        grid=(nseq // nb, L // lt),
        in_specs=[seq3(A_COLS), pl.BlockSpec((nb, 1, A_COLS), lambda s, t: (s, 0, 0)), state,
                  vec('mu_a'), vec('w0_a'), vec('w2_a'), vec('a0_a'), vec('a2_a'), vec('g2_a'),
                  vec('k_k'), vec('k_a'), vec('r_k'), vec('lnx_w'), vec('lnx_b')],
        out_specs=[seq3(W_A), state, pl.BlockSpec((nb, 1, A_COLS), lambda s, t: (s, 0, 0))],
        out_shape=[jax.ShapeDtypeStruct((nseq, L, W_A), F32),
                   jax.ShapeDtypeStruct((nseq, nquad, N_A, QUAD), F32),
                   jax.ShapeDtypeStruct((nseq, 1, A_COLS), F32)],
        scratch_shapes=[pltpu.VMEM((nb * nquad, N_A, QUAD), F32), pltpu.VMEM((nb, 1, A_COLS), F32),
                        slab(lt + 8), pltpu.VMEM((2 * nb * nquad, (lt + 1) * 8, LANES), F32)]
                       + [slab(lt) for _ in range(5)] + [tile(), tile()],
        compiler_params=_cparams(("parallel", "arbitrary")),
        name="rwkv",
    )(pa, shift0[:, None, :], _pack_wkv(wkv0), P['mu_a'], P['w0_a'], P['w2_a'], P['a0_a'],
      P['a2_a'], P['g2_a'], P['k_k'], P['k_a'], P['r_k'], P['lnx_w'], P['lnx_b'])
    return out, _unpack_wkv(sT), shiftT[:, 0, :]


def _swap_halves(x):
    lane = lax.broadcasted_iota(jnp.int32, x.shape, 1)
    first = (lane % DK_B) < (DK_B // 2)
    return jnp.where(first, pltpu.roll(x, QK_B - DK_B // 2, axis=1), pltpu.roll(x, DK_B // 2, axis=1))


def _retention_kernel(pb_ref, r0_ref, cos_ref, sin_ref, dmat_ref, rowdec_ref, keydec_ref, chdec_ref,
                      out_ref, rT_ref, r_scr):
    ci = pl.program_id(1)

    @pl.when(ci == 0)
    def _():
        r_scr[...] = r0_ref[...]

    x = pb_ref[...]
    q = x[:, 0:QK_B]
    k = x[:, QK_B:2 * QK_B]
    q = q * cos_ref[...] + _swap_halves(q) * sin_ref[...]
    k = (k * cos_ref[...] + _swap_halves(k) * sin_ref[...]) * (DK_B ** -0.5)
    for h in range(H_B):
        qh = q[:, h * DK_B:(h + 1) * DK_B]
        kh = k[:, h * DK_B:(h + 1) * DK_B]
        vh = x[:, 2 * QK_B + h * DV_B:2 * QK_B + (h + 1) * DV_B]
        gh = x[:, 2 * QK_B + W_B + h * DV_B:2 * QK_B + W_B + (h + 1) * DV_B]
        rs = r_scr[h]
        s = _bdot_nt(qh, kh) * dmat_ref[h]
        o = _bdot(s, vh) + _bdot(qh, rs) * rowdec_ref[h]
        r_scr[h] = rs * chdec_ref[h] + _bdot_tn(kh * keydec_ref[h], vh)
        o = o * lax.rsqrt(jnp.mean(o * o, axis=-1, keepdims=True) + NORM_EPS)
        out_ref[:, h * DV_B:(h + 1) * DV_B] = o * jax.nn.silu(gh)

    @pl.when(ci == pl.num_programs(1) - 1)
    def _():
        rT_ref[...] = r_scr[...]


def _retention_tables(L, lc, pos0):
    lg = jnp.log1p(-jnp.exp2(-5.0 - jnp.arange(H_B, dtype=F32)))
    idx = jnp.arange(lc, dtype=F32)
    diff = idx[:, None] - idx[None, :]
    dmat = jnp.where(diff >= 0, jnp.exp(jnp.maximum(diff, 0.0)[None] * lg[:, None, None]), 0.0)
    rowdec = jnp.exp((idx[None, :] + 1.0) * lg[:, None])[..., None]
    keydec = jnp.exp((lc - 1.0 - idx)[None, :] * lg[:, None])[..., None]
    chdec = jnp.exp(lc * lg)[:, None, None]
    half = DK_B // 2
    inv = ROPE_BASE ** (-jnp.arange(half, dtype=F32) / half)
    ang = (pos0 + jnp.arange(L)).astype(F32)[:, None] * inv[None, :]
    cos, sin = jnp.cos(ang), jnp.sin(ang)
    cos_t = jnp.tile(jnp.concatenate([cos, cos], axis=1), (1, H_B))
    sin_t = jnp.tile(jnp.concatenate([-sin, sin], axis=1), (1, H_B))
    return cos_t, sin_t, dmat, rowdec, keydec, chdec


def retention(pb, ret0, pos0):
    nseq, L, _ = pb.shape
    lc = min(RET_CHUNK, L)
    cos_t, sin_t, dmat, rowdec, keydec, chdec = _retention_tables(L, lc, pos0)
    state = pl.BlockSpec((None, H_B, DK_B, DV_B), lambda s, c: (s, 0, 0, 0))
    tab = pl.BlockSpec((lc, QK_B), lambda s, c: (c, 0))
    out, rT = pl.pallas_call(
        _retention_kernel,
        grid=(nseq, L // lc),
        in_specs=[pl.BlockSpec((None, lc, B_COLS), lambda s, c: (s, c, 0)), state, tab, tab,
                  _full((H_B, lc, lc)), _full((H_B, lc, 1)), _full((H_B, lc, 1)), _full((H_B, 1, 1))],
        out_specs=[pl.BlockSpec((None, lc, W_B), lambda s, c: (s, c, 0)), state],
        out_shape=[jax.ShapeDtypeStruct((nseq, L, W_B), F32),
                   jax.ShapeDtypeStruct((nseq, H_B, DK_B, DV_B), F32)],
        scratch_shapes=[pltpu.VMEM((H_B, DK_B, DV_B), F32)],
        compiler_params=_cparams(("parallel", "arbitrary")),
        name="retention",
    )(pb, ret0, cos_t, sin_t, dmat, rowdec, keydec, chdec)
    return out, rT


def _lam(lq1, lk1, lq2, lk2, lam_init):
    return (jnp.exp(jnp.sum(lq1 * lk1, axis=-1, keepdims=True))
            - jnp.exp(jnp.sum(lq2 * lk2, axis=-1, keepdims=True)) + lam_init)


def _two_map_queries(q):
    lane = lax.broadcasted_iota(jnp.int32, q.shape, 1)
    q = q * (DH_C ** -0.5)
    return jnp.concatenate([jnp.where(lane < DH_C, q, 0.0), jnp.where(lane < DH_C, 0.0, q)],
                           axis=0).astype(BF16)


def _softmax_update(s, pv, m_ref, l_ref, acc_ref, rows=slice(None)):
    m_old = m_ref[rows]
    m_new = jnp.maximum(m_old, jnp.max(s, axis=-1, keepdims=True))
    alpha = jnp.exp(m_old - m_new)
    p = jnp.exp(s - m_new[:, 0:1])
    l_ref[rows] = alpha * l_ref[rows] + jnp.sum(p, axis=-1, keepdims=True)
    acc_ref[rows] = alpha * acc_ref[rows] + pv(p)
    m_ref[rows] = m_new


def _attn_prompt_kernel(lam_init, has_prefix, qi_ref, kj_ref, q_ref, k_ref, v_ref, *rest):
    if has_prefix:
        kp_ref, vp_ref, *rest = rest
    lq1, lk1, lq2, lk2, subln_ref, out_ref, qb_scr, m_scr, acc_scr = rest
    step = pl.program_id(2)
    qi, kj = qi_ref[step], kj_ref[step]
    bq = q_ref.shape[0]

    def process(k, v, mask):
        k = k.astype(BF16)
        v = jnp.concatenate([v.astype(BF16), jnp.ones(v.shape, BF16)], axis=1)
        rc = min(ATTN_ROWS, bq)
        for r0 in range(0, 2 * bq, rc):
            rows = slice(r0, r0 + rc)
            s = _bdot_nt(qb_scr[rows], k)
            if mask:
                r = lax.broadcasted_iota(jnp.int32, s.shape, 0) + (r0 % bq)
                c = lax.broadcasted_iota(jnp.int32, s.shape, 1)
                s = jnp.where(c <= r, s, NEG_INF)
            m_old = m_scr[rows]
            m_new = jnp.maximum(m_old, jnp.max(s, axis=-1, keepdims=True))
            alpha = jnp.exp(m_old - m_new)
            p = jnp.exp(s - m_new[:, 0:1])
            acc_scr[rows] = jnp.concatenate([alpha, alpha], axis=1) * acc_scr[rows] + _bdot(p, v)
            m_scr[rows] = m_new

    @pl.when(kj == 0)
    def _():
        qb_scr[...] = _two_map_queries(q_ref[...])
        m_scr[...] = jnp.full(m_scr.shape, NEG_INF, F32)
        acc_scr[...] = jnp.zeros(acc_scr.shape, F32)
        if has_prefix:
            process(kp_ref[...], vp_ref[...], False)

    @pl.when(kj < qi)
    def _():
        process(k_ref[...], v_ref[...], False)

    @pl.when(kj == qi)
    def _():
        process(k_ref[...], v_ref[...], True)
        lam = _lam(lq1[...], lk1[...], lq2[...], lk2[...], lam_init)
        o = (acc_scr[0:bq, 0:DV_C] / acc_scr[0:bq, DV_C:2 * DV_C]
             - lam * (acc_scr[bq:2 * bq, 0:DV_C] / acc_scr[bq:2 * bq, DV_C:2 * DV_C]))
        out_ref[...] = _rms(o, subln_ref[...], SUBLN_EPS) * (1.0 - lam_init)


def attn_prompt(q, k, v, prefix, P, layer, lam_init):
    nseq, L, _ = q.shape
    blk = min(ATTN_BLOCK, L)
    nblk = L // blk
    pairs = [(i, j) for i in range(nblk) for j in range(i + 1)]
    qi_tab = jnp.array([p[0] for p in pairs], jnp.int32)
    kj_tab = jnp.array([p[1] for p in pairs], jnp.int32)
    qspec = pl.BlockSpec((None, blk, DV_C), lambda b, h, s, qi, kj: (b, qi[s], h))
    kspec = pl.BlockSpec((None, blk, DV_C), lambda b, h, s, qi, kj: (b, kj[s], h))
    args, specs = [q, k, v], [qspec, kspec, kspec]
    if prefix is not None:
        npre = prefix[0].shape[1]
        pspec = pl.BlockSpec((None, npre, DV_C), lambda b, h, s, qi, kj: (b, 0, h))
        args += list(prefix)
        specs += [pspec, pspec]
    lspec = lambda c: pl.BlockSpec((None, 1, c), lambda b, h, s, qi, kj: (layer, 0, 0))
    for name in ('lam_q1', 'lam_k1', 'lam_q2', 'lam_k2'):
        args.append(P[name])
        specs.append(lspec(DH_C))
    args.append(P['subln'])
    specs.append(lspec(DV_C))
    wide = lambda dt: pltpu.VMEM((2 * blk, DV_C), dt)
    grid_spec = pltpu.PrefetchScalarGridSpec(
        num_scalar_prefetch=2,
        grid=(nseq, H_C, len(pairs)),
        in_specs=specs,
        out_specs=qspec,
        scratch_shapes=[wide(BF16), wide(F32), pltpu.VMEM((2 * blk, 2 * DV_C), F32)],
    )
    return pl.pallas_call(
        lambda *refs: _attn_prompt_kernel(lam_init, prefix is not None, *refs),
        grid_spec=grid_spec,
        out_shape=jax.ShapeDtypeStruct((nseq, L, W_C), F32),
        compiler_params=_cparams(("parallel", "parallel", "arbitrary")),
        name="attn_prompt",
    )(qi_tab, kj_tab, *args)


def _attn_sample_kernel(lam_init, npp, pt_ref, q_ref, kn_ref, vn_ref, lq1, lk1, lq2, lk2, subln_ref,
                        *rest):
    k_refs, v_refs = rest[:npp], rest[npp:2 * npp]
    out_ref, qb_scr, m_scr, l_scr, acc_scr = rest[2 * npp:]
    j = pl.program_id(1)
    nq = q_ref.shape[0]
    hr = 2 * nq
    head = lambda h: slice(h * DV_C, (h + 1) * DV_C)
    rows = lambda h: slice(h * hr, (h + 1) * hr)

    @pl.when(j == 0)
    def _():
        for h in range(H_C):
            qb_scr[rows(h), :] = _two_map_queries(q_ref[:, head(h)])
        m_scr[...] = jnp.full(m_scr.shape, NEG_INF, F32)
        l_scr[...] = jnp.zeros(l_scr.shape, F32)
        acc_scr[...] = jnp.zeros(acc_scr.shape, F32)

    of_head = lambda ref, h: ref[pl.ds(h, PAGE_SIZE, stride=H_C), :]
    s = jnp.concatenate(
        [jnp.concatenate([_bdot_nt(qb_scr[rows(h), :], of_head(k_refs[i], h)) for i in range(npp)], axis=1)
         for h in range(H_C)], axis=0)

    def pv_pages(p):
        outs = []
        for h in range(H_C):
            tot = None
            for i in range(npp):
                t = _bdot(p[rows(h), i * PAGE_SIZE:(i + 1) * PAGE_SIZE], of_head(v_refs[i], h))
                tot = t if tot is None else tot + t
            outs.append(tot)
        return jnp.concatenate(outs, axis=0)

    _softmax_update(s, pv_pages, m_scr, l_scr, acc_scr)

    @pl.when(j == pl.num_programs(1) - 1)
    def _():
        sn = jnp.concatenate([_bdot_nt(qb_scr[rows(h), :], kn_ref[:, head(h)]) for h in range(H_C)], axis=0)
        r = lax.broadcasted_iota(jnp.int32, sn.shape, 0)
        c = lax.broadcasted_iota(jnp.int32, sn.shape, 1)
        sn = jnp.where(c <= r % nq, sn, NEG_INF)
        _softmax_update(
            sn, lambda p: jnp.concatenate([_bdot(p[rows(h), :], vn_ref[:, head(h)]) for h in range(H_C)], axis=0),
            m_scr, l_scr, acc_scr)
        lam = _lam(lq1[...], lk1[...], lq2[...], lk2[...], lam_init)
        o_all = acc_scr[...] / l_scr[...]
        for h in range(H_C):
            o = o_all[h * hr:h * hr + nq] - lam * o_all[h * hr + nq:(h + 1) * hr]
            out_ref[:, head(h)] = _rms(o, subln_ref[...], SUBLN_EPS) * (1.0 - lam_init)


def attn_sample(q, k, v, cache_k, cache_v, page_table, P, layer, lam_init):
    nseq, nq, _ = q.shape
    n_pages = page_table.shape[1]
    npp = PAGES_PER_STEP
    seq = pl.BlockSpec((None, nq, W_C), lambda b, j, pt: (b, 0, 0))

    n_pool = cache_k.shape[1]
    cache_k = cache_k.reshape(DEPTH, n_pool, PAGE_SIZE * H_C, DV_C)
    cache_v = cache_v.reshape(DEPTH, n_pool, PAGE_SIZE * H_C, DV_C)

    def page(i):
        return pl.BlockSpec((None, None, PAGE_SIZE * H_C, DV_C),
                            lambda b, j, pt: (layer, pt[b * n_pages + j * npp + i], 0, 0))

    lspec = lambda c: pl.BlockSpec((None, 1, c), lambda b, j, pt: (layer, 0, 0))
    rows = H_C * 2 * nq
    grid_spec = pltpu.PrefetchScalarGridSpec(
        num_scalar_prefetch=1,
        grid=(nseq, n_pages // npp),
        in_specs=[seq, seq, seq, lspec(DH_C), lspec(DH_C), lspec(DH_C), lspec(DH_C), lspec(DV_C)]
                 + [page(i) for i in range(npp)] + [page(i) for i in range(npp)],
        out_specs=seq,
        scratch_shapes=[pltpu.VMEM((rows, DV_C), BF16), pltpu.VMEM((rows, DV_C), F32),
                        pltpu.VMEM((rows, DV_C), F32), pltpu.VMEM((rows, DV_C), F32)],
    )
    return pl.pallas_call(
        lambda *refs: _attn_sample_kernel(lam_init, npp, *refs),
        grid_spec=grid_spec,
        out_shape=jax.ShapeDtypeStruct((nseq, nq, W_C), F32),
        compiler_params=_cparams(("parallel", "arbitrary")),
        name="attn_sample",
    )(page_table.reshape(-1), q, k, v, P['lam_q1'], P['lam_k1'], P['lam_q2'], P['lam_k2'], P['subln'],
      *([cache_k] * npp), *([cache_v] * npp))


def _merge_kernel(x_ref, oa_ref, ob_ref, oc_ref, g_ref, wg_ref, wpa_ref, wpb_ref, wpc_ref, wo_ref,
                  out_ref):
    x = x_ref[...]
    h = _rms(x, g_ref[...], NORM_EPS).astype(BF16)
    merged = None
    for i, (o_ref, w_ref) in enumerate(((oa_ref, wpa_ref), (ob_ref, wpb_ref), (oc_ref, wpc_ref))):
        gate = jax.nn.sigmoid(jnp.dot(h, wg_ref[:, i * D_MODEL:(i + 1) * D_MODEL],
                                      preferred_element_type=F32))
        term = gate * jnp.dot(o_ref[...].astype(BF16), w_ref[...], preferred_element_type=F32)
        merged = term if merged is None else merged + term
    out_ref[...] = x + jnp.dot(merged.astype(BF16), wo_ref[...], preferred_element_type=F32)


def merge(x, oa, ob, oc, P, layer):
    n = x.shape[0]
    tm = min(ROW_TILE, n)
    row = lambda c: pl.BlockSpec((tm, c), lambda i: (i, 0))
    return pl.pallas_call(
        _merge_kernel,
        grid=(n // tm,),
        in_specs=[row(D_MODEL), row(W_A), row(W_B), row(W_C), _layer((1, D_MODEL), layer),
                  _layer((D_MODEL, 3 * D_MODEL), layer), _layer((W_A, D_MODEL), layer),
                  _layer((W_B, D_MODEL), layer), _layer((W_C, D_MODEL), layer),
                  _layer((D_MODEL, D_MODEL), layer)],
        out_specs=row(D_MODEL),
        out_shape=jax.ShapeDtypeStruct((n, D_MODEL), F32),
        compiler_params=_cparams(("parallel",)),
        name="merge",
    )(x, oa, ob, oc, P['norm1'], P['w_gate'], P['w_pa'], P['w_pb'], P['w_pc'], P['w_o'])


def _ffn_kernel(x_ref, c0_ref, g_ref, win_ref, cw_ref, cb_ref, wout_ref, out_ref, cT_ref, carry_scr):
    sb, tl = x_ref.shape[0], x_ref.shape[1]
    ti = pl.program_id(1)

    @pl.when(ti == 0)
    def _():
        carry_scr[...] = c0_ref[...]

    x = x_ref[...].reshape(sb * tl, D_MODEL)
    h = _rms(x, g_ref[...], NORM_EPS).astype(BF16)
    u = jnp.dot(h, win_ref[:, 0:D_FF], preferred_element_type=F32)
    gate = jnp.dot(h, win_ref[:, D_FF:2 * D_FF], preferred_element_type=F32)
    t = lax.broadcasted_iota(jnp.int32, (sb, tl, D_FF), 1).reshape(sb * tl, D_FF)
    bc = lambda j: jnp.broadcast_to(carry_scr[:, j:j + 1, :], (sb, tl, D_FF)).reshape(sb * tl, D_FF)
    prev1 = jnp.where(t == 0, bc(1), pltpu.roll(u, 1, axis=0))
    prev2 = jnp.where(t == 0, bc(0), jnp.where(t == 1, bc(1), pltpu.roll(u, 2, axis=0)))
    u3 = u.reshape(sb, tl, D_FF)
    carry_scr[...] = u3[:, tl - 2:tl, :]
    uc = cb_ref[...] + prev2 * cw_ref[0:1, :]
    uc = uc + prev1 * cw_ref[1:2, :]
    uc = uc + u * cw_ref[2:3, :]
    act = jax.nn.gelu(uc) * gate
    y = x + jnp.dot(act.astype(BF16), wout_ref[...], preferred_element_type=F32)
    out_ref[...] = y.reshape(sb, tl, D_MODEL)

    @pl.when(ti == pl.num_programs(1) - 1)
    def _():
        cT_ref[...] = carry_scr[...]


def ffn(x, conv0, P, layer, sb, tl):
    nseq, L, _ = x.shape
    blk = pl.BlockSpec((sb, tl, D_MODEL), lambda s, t: (s, t, 0))
    cst = pl.BlockSpec((sb, CONV_W - 1, D_FF), lambda s, t: (s, 0, 0))
    return pl.pallas_call(
        _ffn_kernel,
        grid=(nseq // sb, L // tl),
        in_specs=[blk, cst, _layer((1, D_MODEL), layer), _layer((D_MODEL, 2 * D_FF), layer),
                  _layer((CONV_W, D_FF), layer), _layer((1, D_FF), layer), _layer((D_FF, D_MODEL), layer)],
        out_specs=[blk, cst],
        out_shape=[jax.ShapeDtypeStruct((nseq, L, D_MODEL), F32),
                   jax.ShapeDtypeStruct((nseq, CONV_W - 1, D_FF), F32)],
        scratch_shapes=[pltpu.VMEM((sb, CONV_W - 1, D_FF), F32)],
        compiler_params=_cparams(("parallel", "arbitrary")),
        name="ffn",
    )(x, conv0, P['norm2'], P['w_ffn_in'], P['conv_w'], P['conv_b'], P['w_ffn_out'])


def _norm_kernel(x_ref, g_ref, out_ref):
    out_ref[...] = _rms(x_ref[...], g_ref[...], NORM_EPS)


def final_norm(x, gain):
    n = x.shape[0]
    tm = min(ROW_TILE, n)
    row = pl.BlockSpec((tm, D_MODEL), lambda i: (i, 0))
    return pl.pallas_call(
        _norm_kernel,
        grid=(n // tm,),
        in_specs=[row, _full((1, D_MODEL))],
        out_specs=row,
        out_shape=jax.ShapeDtypeStruct((n, D_MODEL), F32),
        compiler_params=_cparams(("parallel",)),
        name="final_norm",
    )(x, gain)


def _prepare_params(W):
    P = {}
    w_in = W['w_in']
    P['w_a'] = w_in[:, :, 0:A_COLS].astype(BF16)
    P['w_b'] = w_in[:, :, A_COLS:A_COLS + B_COLS].astype(BF16)
    P['w_c'] = w_in[:, :, A_COLS + B_COLS:A_COLS + B_COLS + C_COLS].astype(BF16)
    P['w_gate'] = w_in[:, :, A_COLS + B_COLS + C_COLS:].astype(BF16)
    for name in ('w_pa', 'w_pb', 'w_pc', 'w_o', 'w_ffn_in', 'w_ffn_out'):
        P[name] = W[name].astype(BF16)
    for name in ('norm1', 'norm2', 'mu_a', 'w0_a', 'a0_a', 'k_k', 'k_a', 'lnx_w', 'lnx_b', 'conv_b',
                 'lam_q1', 'lam_k1', 'lam_q2', 'lam_k2', 'subln'):
        P[name] = W[name][:, None, :]
    P['r_k'] = W['r_k'].reshape(DEPTH, 1, W_A)
    P['w2_a'] = jnp.concatenate([W['w2_a'], jnp.zeros((DEPTH, R_A, W_A), F32)], axis=1)
    P['a2_a'] = jnp.concatenate([jnp.zeros((DEPTH, R_W, W_A), F32), W['a2_a']], axis=1)
    P['g2_a'] = W['g2_a']
    P['conv_w'] = W['conv_w']
    return P


def _layer_group(x, states, pos0, attn_fn, P, layer, ffn_blk):
    nseq, L, _ = x.shape
    wkv0, shift0, ret0, conv0 = states
    flat = lambda z: z.reshape(nseq * L, z.shape[-1])
    seq = lambda z: z.reshape(nseq, L, z.shape[-1])
    pa, pb, q, k, v = in_proj(flat(x), P['norm1'], P['w_a'], P['w_b'], P['w_c'], layer)
    out_a, wkv_t, shift_t = rwkv(seq(pa), shift0, wkv0, P, layer)
    out_b, ret_t = retention(seq(pb), ret0, pos0)
    lam_init = 0.8 - 0.6 * math.exp(-0.3 * layer)
    q, k, v = seq(q), seq(k), seq(v)
    out_c = attn_fn(q, k, v, lam_init)
    x = merge(flat(x), flat(out_a), flat(out_b), flat(out_c), P, layer)
    x, conv_t = ffn(seq(x), conv0, P, layer, *ffn_blk)
    return x, (k, v, wkv_t, shift_t, ret_t, conv_t)


def kernel(x_prompt, x_sample, cache_k, cache_v, page_table, state_wkv, state_shift, state_ret,
           state_conv, meta, norm1, w_in, mu_a, w0_a, w2_a, a0_a, a2_a, g2_a, k_k, k_a, r_k,
           lnx_w, lnx_b, lam_q1, lam_k1, lam_q2, lam_k2, subln, w_pa, w_pb, w_pc, w_o, norm2,
           w_ffn_in, conv_w, conv_b, w_ffn_out, norm_f):
    W = dict(norm1=norm1, w_in=w_in, mu_a=mu_a, w0_a=w0_a, w2_a=w2_a, a0_a=a0_a, a2_a=a2_a,
             g2_a=g2_a, k_k=k_k, k_a=k_a, r_k=r_k, lnx_w=lnx_w, lnx_b=lnx_b, lam_q1=lam_q1,
             lam_k1=lam_k1, lam_q2=lam_q2, lam_k2=lam_k2, subln=subln, w_pa=w_pa, w_pb=w_pb,
             w_pc=w_pc, w_o=w_o, norm2=norm2, w_ffn_in=w_ffn_in, conv_w=conv_w, conv_b=conv_b,
             w_ffn_out=w_ffn_out)
    P = _prepare_params(W)
    b, t_main, _ = x_prompt.shape
    bd, ds, _ = x_sample.shape
    past = page_table.shape[1] * PAGE_SIZE

    x_meta = jnp.broadcast_to(meta[None].astype(F32), (b, N_META, D_MODEL))
    x_main = x_prompt
    x_samp = x_sample
    zeros = lambda *s: jnp.zeros(s, F32)
    meta_states = (zeros(b, H_A, N_A, N_A), zeros(b, A_COLS), zeros(b, H_B, DK_B, DV_B),
                   zeros(b, CONV_W - 1, D_FF))
    outs_p = [[] for _ in range(6)]
    outs_s = [[] for _ in range(6)]
    for l in range(DEPTH):
        x_meta, st_meta = _layer_group(
            x_meta, meta_states, 0,
            lambda q, k, v, li: attn_prompt(q, k, v, None, P, l, li), P, l, (b, N_META))
        km, vm = st_meta[0], st_meta[1]
        x_main, st_main = _layer_group(
            x_main, st_meta[2:], N_META,
            lambda q, k, v, li: attn_prompt(q, k, v, (km, vm), P, l, li), P, l,
            (1, min(ROW_TILE, t_main)))
        x_samp, st_samp = _layer_group(
            x_samp, (state_wkv[l], state_shift[l], state_ret[l], state_conv[l]), past,
            lambda q, k, v, li: attn_sample(q, k, v, cache_k, cache_v, page_table, P, l, li), P, l,
            (bd, ds))
        heads = lambda z: z.reshape(z.shape[0], z.shape[1], H_C, DV_C)
        outs_p[0].append(jnp.concatenate([heads(km), heads(st_main[0])], axis=1))
        outs_p[1].append(jnp.concatenate([heads(vm), heads(st_main[1])], axis=1))
        for i in range(2, 6):
            outs_p[i].append(st_main[i])
        outs_s[0].append(heads(st_samp[0]))
        outs_s[1].append(heads(st_samp[1]))
        for i in range(2, 6):
            outs_s[i].append(st_samp[i])
    gain_f = norm_f[None, :]
    y_prompt = final_norm(x_main.reshape(b * t_main, D_MODEL), gain_f).reshape(b, t_main, D_MODEL)
    y_sample = final_norm(x_samp.reshape(bd * ds, D_MODEL), gain_f).reshape(bd, ds, D_MODEL)
    stack = lambda lst: jnp.stack(lst, 0)
    return (y_prompt, y_sample, *[stack(o) for o in outs_p], *[stack(o) for o in outs_s])
```

```python
import math

import jax
import jax.numpy as jnp
from jax import lax
from jax.experimental import pallas as pl
from jax.experimental.pallas import tpu as pltpu

F32 = jnp.float32
BF16 = jnp.bfloat16

D_MODEL = 1024
DEPTH = 4
N_META = 16
H_A, N_A = 8, 64
W_A = H_A * N_A
R_W, R_A, R_G = 64, 64, 128
H_B, DK_B, DV_B = 4, 64, 128
QK_B, W_B = H_B * DK_B, H_B * DV_B
ROPE_BASE = 10000.0
H_C, DH_C = 4, 64
DV_C = 2 * DH_C
W_C = H_C * DV_C
D_FF = 2816
CONV_W = 3
A_COLS = 3 * W_A + R_W + R_A + R_G
B_COLS = 2 * QK_B + 2 * W_B
C_COLS = 3 * W_C
NORM_EPS = 1e-6
GN_EPS_A = 64e-5
SUBLN_EPS = 1e-5
NEG_INF = -1e30
PAGE_SIZE = 128

LANES = 128
VMEM_LIMIT = 52 * 1024 * 1024
ROW_TILE = 512
RWKV_SEQS = 4
RWKV_TILE = 128
RWKV_GROUP = 4
RWKV_UNROLL = 8
RET_CHUNK = 128
RET_CHUNKS_PER_STEP = 4
ATTN_BLOCK = 1024
ATTN_ROWS = 256
PAGES_PER_STEP = 16

HIGHEST = lax.Precision.HIGHEST


def _cparams(sem):
    return pltpu.CompilerParams(dimension_semantics=sem, vmem_limit_bytes=VMEM_LIMIT)


def _rms(x, gain, eps):
    return x * lax.rsqrt(jnp.mean(x * x, axis=-1, keepdims=True) + eps) * gain


def _bdot(a, b):
    return jnp.dot(a.astype(BF16), b.astype(BF16), preferred_element_type=F32)


def _bdot_nt(a, b):
    return lax.dot_general(a.astype(BF16), b.astype(BF16), (((1,), (1,)), ((), ())),
                           preferred_element_type=F32)


def _bdot_tn(a, b):
    return lax.dot_general(a.astype(BF16), b.astype(BF16), (((0,), (0,)), ((), ())),
                           preferred_element_type=F32)


def _fdot(a, b):
    return jnp.dot(a, b, precision=HIGHEST, preferred_element_type=F32)


def _full(shape):
    return pl.BlockSpec(shape, lambda *_: (0,) * len(shape))


def _layer(shape, layer):
    return pl.BlockSpec((None,) + shape, lambda *_: (layer,) + (0,) * len(shape))


def _in_proj_kernel(x_ref, g_ref, wa_ref, wb_ref, wc_ref, pa_ref, pb_ref, q_ref, k_ref, v_ref):
    h = _rms(x_ref[...], g_ref[...], NORM_EPS).astype(BF16)
    pa_ref[...] = jnp.dot(h, wa_ref[...], preferred_element_type=F32)
    pb_ref[...] = jnp.dot(h, wb_ref[...], preferred_element_type=F32)
    q_ref[...] = jnp.dot(h, wc_ref[:, 0:W_C], preferred_element_type=F32)
    k_ref[...] = jnp.dot(h, wc_ref[:, W_C:2 * W_C], preferred_element_type=F32)
    v_ref[...] = jnp.dot(h, wc_ref[:, 2 * W_C:3 * W_C], preferred_element_type=F32)


def in_proj(x, gain, wa, wb, wc, layer):
    n = x.shape[0]
    tm = min(ROW_TILE, n)
    row = lambda c: pl.BlockSpec((tm, c), lambda i: (i, 0))
    return pl.pallas_call(
        _in_proj_kernel,
        grid=(n // tm,),
        in_specs=[row(D_MODEL), _layer((1, D_MODEL), layer), _layer((D_MODEL, A_COLS), layer),
                  _layer((D_MODEL, B_COLS), layer), _layer((D_MODEL, C_COLS), layer)],
        out_specs=[row(A_COLS), row(B_COLS), row(W_C), row(W_C), row(W_C)],
        out_shape=[jax.ShapeDtypeStruct((n, c), F32) for c in (A_COLS, B_COLS, W_C, W_C, W_C)],
        compiler_params=_cparams(("parallel",)),
        name="in_proj",
    )(x, gain, wa, wb, wc)


QUAD = 4 * N_A


def _block_ones(n, dtype):
    r = lax.broadcasted_iota(jnp.int32, (n, n), 0) // N_A
    c = lax.broadcasted_iota(jnp.int32, (n, n), 1) // N_A
    return (r == c).astype(dtype)


def _head_sum(x, ones):
    hi = x.astype(BF16)
    lo = (x - hi.astype(F32)).astype(BF16)
    return (jnp.dot(hi, ones, preferred_element_type=F32)
            + jnp.dot(lo, ones, preferred_element_type=F32))


def _rwkv_kernel(pa_ref, shift0_ref, s0_ref, mu_ref, w0_ref, w2_ref, a0_ref, a2_ref, g2_ref,
                 kk_ref, ka_ref, rk_ref, lnw_ref, lnb_ref,
                 out_ref, sT_ref, shiftT_ref,
                 s_scr, carry_scr, r_scr, y_scr, k_scr, v_scr, d_scr, al_scr, be_scr, g_scr, bonus_scr):
    nb, lt = pa_ref.shape[0], pa_ref.shape[1]
    nquad = W_A // QUAD
    ti = pl.program_id(1)

    @pl.when(ti == 0)
    def _():
        s_scr[...] = s0_ref[...].reshape(nb * nquad, N_A, QUAD)
        carry_scr[...] = shift0_ref[...]

    ones = _block_ones(W_A, BF16)
    row = lax.broadcasted_iota(jnp.int32, (lt, A_COLS), 0)
    for b in range(nb):
        x = pa_ref[b]
        prev = jnp.where(row == 0, carry_scr[b], pltpu.roll(x, 1, axis=0))
        carry_scr[b] = x[lt - 1:lt, :]
        xm = x + (prev - x) * mu_ref[...]
        r = xm[:, 0:W_A]
        k = xm[:, W_A:2 * W_A]
        v = xm[:, 2 * W_A:3 * W_A]
        lr = xm[:, 3 * W_A:3 * W_A + R_W + R_A]
        gd = xm[:, 3 * W_A + R_W + R_A:A_COLS]
        w = -jax.nn.softplus(-(w0_ref[...] + _fdot(jnp.tanh(lr), w2_ref[...]))) - 0.5
        a = jax.nn.sigmoid(a0_ref[...] + _fdot(lr, a2_ref[...]))
        kk = k * kk_ref[...]
        kk = kk * lax.rsqrt(jnp.maximum(_head_sum(kk * kk, ones), 1e-24))
        k = k * (1.0 + (a - 1.0) * ka_ref[...])
        g_scr[b] = _bdot(jax.nn.sigmoid(gd), g2_ref[...])
        bonus_scr[b] = _head_sum(r * k * rk_ref[...], ones) * v
        for q in range(nquad):
            i, cols = b * nquad + q, slice(q * QUAD, (q + 1) * QUAD)
            r_scr[i, 0:8, :] = jnp.zeros((8, QUAD), F32)
            r_scr[i, 8:lt + 8, :] = r[:, cols]
            k_scr[i] = k[:, cols]
            v_scr[i] = v[:, cols]
            d_scr[i] = jnp.exp(-jnp.exp(w))[:, cols]
            al_scr[i] = -kk[:, cols]
            be_scr[i] = (kk * a)[:, cols]

    ones_q = _block_ones(QUAD, BF16)
    lane = lax.broadcasted_iota(jnp.int32, (N_A, QUAD), 1)
    sub = lax.broadcasted_iota(jnp.int32, (N_A, QUAD), 0)
    diag = (lane % N_A) == sub

    def to_rows8(z):
        return jnp.sum(jnp.where(diag, z, 0.0).reshape(N_A // 8, 8, QUAD), axis=0)

    def put_y(i, rows, y8):
        y_scr[2 * i, rows, :] = y8[:, 0:LANES]
        y_scr[2 * i + 1, rows, :] = y8[:, LANES:QUAD]

    def step(t, carry):
        for g in range(0, nb * nquad, RWKV_GROUP):
            group = range(g, g + RWKV_GROUP)
            states, parts = [], []
            for i in group:
                s = s_scr[i]
                sb = s.astype(BF16)
                states.append(s)
                parts += [sb * al_scr[i, pl.ds(t, 1), :].astype(BF16),
                          sb * r_scr[i, pl.ds(t + 7, 1), :].astype(BF16),
                          jnp.where(diag, v_scr[i, pl.ds(t, 1), :], 0.0).astype(BF16)]
            o = jnp.dot(jnp.concatenate(parts, axis=0), ones_q, preferred_element_type=F32)
            for j, i in enumerate(group):
                sa, y_prev, vcol = (o[(3 * j + n) * N_A:(3 * j + n + 1) * N_A] for n in range(3))
                s_scr[i] = (states[j] * d_scr[i, pl.ds(t, 1), :] + sa * be_scr[i, pl.ds(t, 1), :]
                            + vcol * k_scr[i, pl.ds(t, 1), :])
                put_y(i, pl.ds(pl.multiple_of(t * 8, 8), 8), to_rows8(y_prev))
        return carry

    lax.fori_loop(0, lt, step, 0, unroll=min(RWKV_UNROLL, lt))

    for b in range(nb):
        for q in range(nquad):
            i = b * nquad + q
            o = jnp.dot((s_scr[i] * r_scr[i, lt + 7:lt + 8, :]).astype(BF16), ones_q,
                        preferred_element_type=F32)
            put_y(i, slice(lt * 8, (lt + 1) * 8), to_rows8(o))

        def half_y(n):
            tot = y_scr[n, pl.ds(8, lt, stride=8), :]
            for j in range(1, 8):
                tot = tot + y_scr[n, pl.ds(8 + j, lt, stride=8), :]
            return tot

        y = jnp.concatenate([half_y(2 * b * nquad + n) for n in range(2 * nquad)], axis=1)
        yc = y - _head_sum(y, ones) * (1.0 / N_A)
        yn = yc * lax.rsqrt(_head_sum(yc * yc, ones) * (1.0 / N_A) + GN_EPS_A)
        out_ref[b] = (yn * lnw_ref[...] + lnb_ref[...] + bonus_scr[b]) * g_scr[b]

    @pl.when(ti == pl.num_programs(1) - 1)
    def _():
        sT_ref[...] = s_scr[...].reshape(nb, nquad, N_A, QUAD)
        shiftT_ref[...] = carry_scr[...]


def _pack_wkv(s):
    n = s.shape[0]
    return s.reshape(n, H_A // 4, 4, N_A, N_A).transpose(0, 1, 3, 2, 4).reshape(n, H_A // 4, N_A, QUAD)


def _unpack_wkv(s):
    n = s.shape[0]
    return s.reshape(n, H_A // 4, N_A, 4, N_A).transpose(0, 1, 3, 2, 4).reshape(n, H_A, N_A, N_A)


def rwkv(pa, shift0, wkv0, P, layer):
    nseq, L, _ = pa.shape
    nb = RWKV_SEQS
    lt = min(RWKV_TILE, L)
    nquad = W_A // QUAD
    seq3 = lambda c: pl.BlockSpec((nb, lt, c), lambda s, t: (s, t, 0))
    vec = lambda name: _layer(P[name].shape[1:], layer)
    state = pl.BlockSpec((nb, nquad, N_A, QUAD), lambda s, t: (s, 0, 0, 0))
    slab = lambda rows: pltpu.VMEM((nb * nquad, rows, QUAD), F32)
    tile = lambda: pltpu.VMEM((nb, lt, W_A), F32)
    out, sT, shiftT = pl.pallas_call(
        _rwkv_kernel,
        grid=(nseq // nb, L // lt),
        in_specs=[seq3(A_COLS), pl.BlockSpec((nb, 1, A_COLS), lambda s, t: (s, 0, 0)), state,
                  vec('mu_a'), vec('w0_a'), vec('w2_a'), vec('a0_a'), vec('a2_a'), vec('g2_a'),
                  vec('k_k'), vec('k_a'), vec('r_k'), vec('lnx_w'), vec('lnx_b')],
        out_specs=[seq3(W_A), state, pl.BlockSpec((nb, 1, A_COLS), lambda s, t: (s, 0, 0))],
        out_shape=[jax.ShapeDtypeStruct((nseq, L, W_A), F32),
                   jax.ShapeDtypeStruct((nseq, nquad, N_A, QUAD), F32),
                   jax.ShapeDtypeStruct((nseq, 1, A_COLS), F32)],
        scratch_shapes=[pltpu.VMEM((nb * nquad, N_A, QUAD), F32), pltpu.VMEM((nb, 1, A_COLS), F32),
                        slab(lt + 8), pltpu.VMEM((2 * nb * nquad, (lt + 1) * 8, LANES), F32)]
                       + [slab(lt) for _ in range(5)] + [tile(), tile()],
        compiler_params=_cparams(("parallel", "arbitrary")),
        name="rwkv",
    )(pa, shift0[:, None, :], _pack_wkv(wkv0), P['mu_a'], P['w0_a'], P['w2_a'], P['a0_a'],
      P['a2_a'], P['g2_a'], P['k_k'], P['k_a'], P['r_k'], P['lnx_w'], P['lnx_b'])
    return out, _unpack_wkv(sT), shiftT[:, 0, :]


def _swap_halves(x):
    lane = lax.broadcasted_iota(jnp.int32, x.shape, 1)
    first = (lane % DK_B) < (DK_B // 2)
    return jnp.where(first, pltpu.roll(x, QK_B - DK_B // 2, axis=1), pltpu.roll(x, DK_B // 2, axis=1))


def _head_blocks():
    r = lax.broadcasted_iota(jnp.int32, (QK_B, W_B), 0) // DK_B
    c = lax.broadcasted_iota(jnp.int32, (QK_B, W_B), 1) // DV_B
    return r == c


def _retention_kernel(pb_ref, r0_ref, cos_ref, sin_ref, dmat_ref, rowdec_ref, keydec_ref, chdec_ref,
                      out_ref, rT_ref, r_scr):
    ci = pl.program_id(1)

    @pl.when(ci == 0)
    def _():
        r_scr[...] = jnp.zeros(r_scr.shape, F32)
        for h in range(H_B):
            r_scr[h * DK_B:(h + 1) * DK_B, h * DV_B:(h + 1) * DV_B] = r0_ref[h]

    lc = dmat_ref.shape[1]
    for c0 in range(0, pb_ref.shape[0], lc):
        rows = slice(c0, c0 + lc)
        x = pb_ref[rows, :]
        q = x[:, 0:QK_B]
        k = x[:, QK_B:2 * QK_B]
        v = x[:, 2 * QK_B:2 * QK_B + W_B]
        q = q * cos_ref[rows, :] + _swap_halves(q) * sin_ref[rows, :]
        k = (k * cos_ref[rows, :] + _swap_halves(k) * sin_ref[rows, :]) * (DK_B ** -0.5)
        state = r_scr[...]
        o_state = _bdot(q, state) * rowdec_ref[...]
        update = _bdot_tn(k * keydec_ref[...], v)
        r_scr[...] = state * chdec_ref[...] + jnp.where(_head_blocks(), update, 0.0)
        q_head = lax.broadcasted_iota(jnp.int32, q.shape, 1) // DK_B
        for h in range(H_B):
            cols = slice(h * DV_B, (h + 1) * DV_B)
            s = _bdot_nt(jnp.where(q_head == h, q, 0.0), k) * dmat_ref[h]
            o = _bdot(s, v[:, cols]) + o_state[:, cols]
            o = o * lax.rsqrt(jnp.mean(o * o, axis=-1, keepdims=True) + NORM_EPS)
            out_ref[rows, cols] = o * jax.nn.silu(x[:, 2 * QK_B + W_B + h * DV_B:2 * QK_B + W_B + (h + 1) * DV_B])

    @pl.when(ci == pl.num_programs(1) - 1)
    def _():
        for h in range(H_B):
            rT_ref[h] = r_scr[h * DK_B:(h + 1) * DK_B, h * DV_B:(h + 1) * DV_B]


def _retention_tables(L, lc, pos0):
    lg = jnp.log1p(-jnp.exp2(-5.0 - jnp.arange(H_B, dtype=F32)))
    idx = jnp.arange(lc, dtype=F32)
    diff = idx[:, None] - idx[None, :]
    dmat = jnp.where(diff >= 0, jnp.exp(jnp.maximum(diff, 0.0)[None] * lg[:, None, None]), 0.0)
    rowdec = jnp.repeat(jnp.exp((idx[:, None] + 1.0) * lg[None, :]), DV_B, axis=1)
    keydec = jnp.repeat(jnp.exp((lc - 1.0 - idx)[:, None] * lg[None, :]), DK_B, axis=1)
    chdec = jnp.broadcast_to(jnp.repeat(jnp.exp(lc * lg), DK_B)[:, None], (QK_B, W_B))
    half = DK_B // 2
    inv = ROPE_BASE ** (-jnp.arange(half, dtype=F32) / half)
    ang = (pos0 + jnp.arange(L)).astype(F32)[:, None] * inv[None, :]
    cos, sin = jnp.cos(ang), jnp.sin(ang)
    cos_t = jnp.tile(jnp.concatenate([cos, cos], axis=1), (1, H_B))
    sin_t = jnp.tile(jnp.concatenate([-sin, sin], axis=1), (1, H_B))
    return cos_t, sin_t, dmat, rowdec, keydec, chdec


def retention(pb, ret0, pos0):
    nseq, L, _ = pb.shape
    lc = min(RET_CHUNK, L)
    cos_t, sin_t, dmat, rowdec, keydec, chdec = _retention_tables(L, lc, pos0)
    state = pl.BlockSpec((None, H_B, DK_B, DV_B), lambda s, c: (s, 0, 0, 0))
    lb = min(RET_CHUNK * RET_CHUNKS_PER_STEP, L)
    tab = pl.BlockSpec((lb, QK_B), lambda s, c: (c, 0))
    out, rT = pl.pallas_call(
        _retention_kernel,
        grid=(nseq, L // lb),
        in_specs=[pl.BlockSpec((None, lb, B_COLS), lambda s, c: (s, c, 0)), state, tab, tab,
                  _full((H_B, lc, lc)), _full((lc, W_B)), _full((lc, QK_B)), _full((QK_B, W_B))],
        out_specs=[pl.BlockSpec((None, lb, W_B), lambda s, c: (s, c, 0)), state],
        out_shape=[jax.ShapeDtypeStruct((nseq, L, W_B), F32),
                   jax.ShapeDtypeStruct((nseq, H_B, DK_B, DV_B), F32)],
        scratch_shapes=[pltpu.VMEM((QK_B, W_B), F32)],
        compiler_params=_cparams(("parallel", "arbitrary")),
        name="retention",
    )(pb, ret0, cos_t, sin_t, dmat, rowdec, keydec, chdec)
    return out, rT


def _lam(lq1, lk1, lq2, lk2, lam_init):
    return (jnp.exp(jnp.sum(lq1 * lk1, axis=-1, keepdims=True))
            - jnp.exp(jnp.sum(lq2 * lk2, axis=-1, keepdims=True)) + lam_init)


def _two_map_queries(q):
    lane = lax.broadcasted_iota(jnp.int32, q.shape, 1)
    q = q * (DH_C ** -0.5)
    return jnp.concatenate([jnp.where(lane < DH_C, q, 0.0), jnp.where(lane < DH_C, 0.0, q)],
                           axis=0).astype(BF16)


def _softmax_update(s, pv, m_ref, l_ref, acc_ref, rows=slice(None)):
    m_old = m_ref[rows]
    m_new = jnp.maximum(m_old, jnp.max(s, axis=-1, keepdims=True))
    alpha = jnp.exp(m_old - m_new)
    p = jnp.exp(s - m_new[:, 0:1])
    l_ref[rows] = alpha * l_ref[rows] + jnp.sum(p, axis=-1, keepdims=True)
    acc_ref[rows] = alpha * acc_ref[rows] + pv(p)
    m_ref[rows] = m_new


def _attn_prompt_kernel(lam_init, has_prefix, qi_ref, kj_ref, q_ref, k_ref, v_ref, *rest):
    if has_prefix:
        kp_ref, vp_ref, *rest = rest
    lq1, lk1, lq2, lk2, subln_ref, out_ref, qb_scr, m_scr, acc_scr = rest
    step = pl.program_id(2)
    qi, kj = qi_ref[step], kj_ref[step]
    bq = q_ref.shape[0]

    def process(k, v, mask):
        k = k.astype(BF16)
        v = jnp.concatenate([v.astype(BF16), jnp.ones(v.shape, BF16)], axis=1)
        rc = min(ATTN_ROWS, bq)
        for r0 in range(0, 2 * bq, rc):
            rows = slice(r0, r0 + rc)
            s = _bdot_nt(qb_scr[rows], k)
            if mask:
                r = lax.broadcasted_iota(jnp.int32, s.shape, 0) + (r0 % bq)
                c = lax.broadcasted_iota(jnp.int32, s.shape, 1)
                s = jnp.where(c <= r, s, NEG_INF)
            m_old = m_scr[rows]
            m_new = jnp.maximum(m_old, jnp.max(s, axis=-1, keepdims=True))
            alpha = jnp.exp(m_old - m_new)
            p = jnp.exp(s - m_new[:, 0:1])
            acc_scr[rows] = jnp.concatenate([alpha, alpha], axis=1) * acc_scr[rows] + _bdot(p, v)
            m_scr[rows] = m_new

    @pl.when(kj == 0)
    def _():
        qb_scr[...] = _two_map_queries(q_ref[...])
        m_scr[...] = jnp.full(m_scr.shape, NEG_INF, F32)
        acc_scr[...] = jnp.zeros(acc_scr.shape, F32)
        if has_prefix:
            process(kp_ref[...], vp_ref[...], False)

    @pl.when(kj < qi)
    def _():
        process(k_ref[...], v_ref[...], False)

    @pl.when(kj == qi)
    def _():
        process(k_ref[...], v_ref[...], True)
        lam = _lam(lq1[...], lk1[...], lq2[...], lk2[...], lam_init)
        o = (acc_scr[0:bq, 0:DV_C] / acc_scr[0:bq, DV_C:2 * DV_C]
             - lam * (acc_scr[bq:2 * bq, 0:DV_C] / acc_scr[bq:2 * bq, DV_C:2 * DV_C]))
        out_ref[...] = _rms(o, subln_ref[...], SUBLN_EPS) * (1.0 - lam_init)


def attn_prompt(q, k, v, prefix, P, layer, lam_init):
    nseq, L, _ = q.shape
    blk = min(ATTN_BLOCK, L)
    nblk = L // blk
    pairs = [(i, j) for i in range(nblk) for j in range(i + 1)]
    qi_tab = jnp.array([p[0] for p in pairs], jnp.int32)
    kj_tab = jnp.array([p[1] for p in pairs], jnp.int32)
    qspec = pl.BlockSpec((None, blk, DV_C), lambda b, h, s, qi, kj: (b, qi[s], h))
    kspec = pl.BlockSpec((None, blk, DV_C), lambda b, h, s, qi, kj: (b, kj[s], h))
    args, specs = [q, k, v], [qspec, kspec, kspec]
    if prefix is not None:
        npre = prefix[0].shape[1]
        pspec = pl.BlockSpec((None, npre, DV_C), lambda b, h, s, qi, kj: (b, 0, h))
        args += list(prefix)
        specs += [pspec, pspec]
    lspec = lambda c: pl.BlockSpec((None, 1, c), lambda b, h, s, qi, kj: (layer, 0, 0))
    for name in ('lam_q1', 'lam_k1', 'lam_q2', 'lam_k2'):
        args.append(P[name])
        specs.append(lspec(DH_C))
    args.append(P['subln'])
    specs.append(lspec(DV_C))
    wide = lambda dt: pltpu.VMEM((2 * blk, DV_C), dt)
    grid_spec = pltpu.PrefetchScalarGridSpec(
        num_scalar_prefetch=2,
        grid=(nseq, H_C, len(pairs)),
        in_specs=specs,
        out_specs=qspec,
        scratch_shapes=[wide(BF16), wide(F32), pltpu.VMEM((2 * blk, 2 * DV_C), F32)],
    )
    return pl.pallas_call(
        lambda *refs: _attn_prompt_kernel(lam_init, prefix is not None, *refs),
        grid_spec=grid_spec,
        out_shape=jax.ShapeDtypeStruct((nseq, L, W_C), F32),
        compiler_params=_cparams(("parallel", "parallel", "arbitrary")),
        name="attn_prompt",
    )(qi_tab, kj_tab, *args)


def _attn_sample_kernel(lam_init, npp, pt_ref, q_ref, kn_ref, vn_ref, lq1, lk1, lq2, lk2, subln_ref,
                        *rest):
    k_refs, v_refs = rest[:npp], rest[npp:2 * npp]
    out_ref, qb_scr, m_scr, l_scr, acc_scr = rest[2 * npp:]
    j = pl.program_id(1)
    nq = q_ref.shape[0]
    hr = 2 * nq
    head = lambda h: slice(h * DV_C, (h + 1) * DV_C)
    rows = lambda h: slice(h * hr, (h + 1) * hr)

    @pl.when(j == 0)
    def _():
        for h in range(H_C):
            qb_scr[rows(h), :] = _two_map_queries(q_ref[:, head(h)])
        m_scr[...] = jnp.full(m_scr.shape, NEG_INF, F32)
        l_scr[...] = jnp.zeros(l_scr.shape, F32)
        acc_scr[...] = jnp.zeros(acc_scr.shape, F32)

    of_head = lambda ref, h: ref[pl.ds(h, PAGE_SIZE, stride=H_C), :]
    s = jnp.concatenate(
        [jnp.concatenate([_bdot_nt(qb_scr[rows(h), :], of_head(k_refs[i], h)) for i in range(npp)], axis=1)
         for h in range(H_C)], axis=0)

    def pv_pages(p):
        outs = []
        for h in range(H_C):
            tot = None
            for i in range(npp):
                t = _bdot(p[rows(h), i * PAGE_SIZE:(i + 1) * PAGE_SIZE], of_head(v_refs[i], h))
                tot = t if tot is None else tot + t
            outs.append(tot)
        return jnp.concatenate(outs, axis=0)

    _softmax_update(s, pv_pages, m_scr, l_scr, acc_scr)

    @pl.when(j == pl.num_programs(1) - 1)
    def _():
        sn = jnp.concatenate([_bdot_nt(qb_scr[rows(h), :], kn_ref[:, head(h)]) for h in range(H_C)], axis=0)
        r = lax.broadcasted_iota(jnp.int32, sn.shape, 0)
        c = lax.broadcasted_iota(jnp.int32, sn.shape, 1)
        sn = jnp.where(c <= r % nq, sn, NEG_INF)
        _softmax_update(
            sn, lambda p: jnp.concatenate([_bdot(p[rows(h), :], vn_ref[:, head(h)]) for h in range(H_C)], axis=0),
            m_scr, l_scr, acc_scr)
        lam = _lam(lq1[...], lk1[...], lq2[...], lk2[...], lam_init)
        o_all = acc_scr[...] / l_scr[...]
        for h in range(H_C):
            o = o_all[h * hr:h * hr + nq] - lam * o_all[h * hr + nq:(h + 1) * hr]
            out_ref[:, head(h)] = _rms(o, subln_ref[...], SUBLN_EPS) * (1.0 - lam_init)


def attn_sample(q, k, v, cache_k, cache_v, page_table, P, layer, lam_init):
    nseq, nq, _ = q.shape
    n_pages = page_table.shape[1]
    npp = PAGES_PER_STEP
    seq = pl.BlockSpec((None, nq, W_C), lambda b, j, pt: (b, 0, 0))

    n_pool = cache_k.shape[1]
    cache_k = cache_k.reshape(DEPTH, n_pool, PAGE_SIZE * H_C, DV_C)
    cache_v = cache_v.reshape(DEPTH, n_pool, PAGE_SIZE * H_C, DV_C)

    def page(i):
        return pl.BlockSpec((None, None, PAGE_SIZE * H_C, DV_C),
                            lambda b, j, pt: (layer, pt[b * n_pages + j * npp + i], 0, 0))

    lspec = lambda c: pl.BlockSpec((None, 1, c), lambda b, j, pt: (layer, 0, 0))
    rows = H_C * 2 * nq
    grid_spec = pltpu.PrefetchScalarGridSpec(
        num_scalar_prefetch=1,
        grid=(nseq, n_pages // npp),
        in_specs=[seq, seq, seq, lspec(DH_C), lspec(DH_C), lspec(DH_C), lspec(DH_C), lspec(DV_C)]
                 + [page(i) for i in range(npp)] + [page(i) for i in range(npp)],
        out_specs=seq,
        scratch_shapes=[pltpu.VMEM((rows, DV_C), BF16), pltpu.VMEM((rows, DV_C), F32),
                        pltpu.VMEM((rows, DV_C), F32), pltpu.VMEM((rows, DV_C), F32)],
    )
    return pl.pallas_call(
        lambda *refs: _attn_sample_kernel(lam_init, npp, *refs),
        grid_spec=grid_spec,
        out_shape=jax.ShapeDtypeStruct((nseq, nq, W_C), F32),
        compiler_params=_cparams(("parallel", "arbitrary")),
        name="attn_sample",
    )(page_table.reshape(-1), q, k, v, P['lam_q1'], P['lam_k1'], P['lam_q2'], P['lam_k2'], P['subln'],
      *([cache_k] * npp), *([cache_v] * npp))


def _merge_kernel(x_ref, oa_ref, ob_ref, oc_ref, g_ref, wg_ref, wpa_ref, wpb_ref, wpc_ref, wo_ref,
                  out_ref):
    x = x_ref[...]
    h = _rms(x, g_ref[...], NORM_EPS).astype(BF16)
    merged = None
    for i, (o_ref, w_ref) in enumerate(((oa_ref, wpa_ref), (ob_ref, wpb_ref), (oc_ref, wpc_ref))):
        gate = jax.nn.sigmoid(jnp.dot(h, wg_ref[:, i * D_MODEL:(i + 1) * D_MODEL],
                                      preferred_element_type=F32))
        term = gate * jnp.dot(o_ref[...].astype(BF16), w_ref[...], preferred_element_type=F32)
        merged = term if merged is None else merged + term
    out_ref[...] = x + jnp.dot(merged.astype(BF16), wo_ref[...], preferred_element_type=F32)


def merge(x, oa, ob, oc, P, layer):
    n = x.shape[0]
    tm = min(ROW_TILE, n)
    row = lambda c: pl.BlockSpec((tm, c), lambda i: (i, 0))
    return pl.pallas_call(
        _merge_kernel,
        grid=(n // tm,),
        in_specs=[row(D_MODEL), row(W_A), row(W_B), row(W_C), _layer((1, D_MODEL), layer),
                  _layer((D_MODEL, 3 * D_MODEL), layer), _layer((W_A, D_MODEL), layer),
                  _layer((W_B, D_MODEL), layer), _layer((W_C, D_MODEL), layer),
                  _layer((D_MODEL, D_MODEL), layer)],
        out_specs=row(D_MODEL),
        out_shape=jax.ShapeDtypeStruct((n, D_MODEL), F32),
        compiler_params=_cparams(("parallel",)),
        name="merge",
    )(x, oa, ob, oc, P['norm1'], P['w_gate'], P['w_pa'], P['w_pb'], P['w_pc'], P['w_o'])


def _ffn_kernel(x_ref, c0_ref, g_ref, win_ref, cw_ref, cb_ref, wout_ref, out_ref, cT_ref, carry_scr):
    sb, tl = x_ref.shape[0], x_ref.shape[1]
    ti = pl.program_id(1)

    @pl.when(ti == 0)
    def _():
        carry_scr[...] = c0_ref[...]

    x = x_ref[...].reshape(sb * tl, D_MODEL)
    h = _rms(x, g_ref[...], NORM_EPS).astype(BF16)
    u = jnp.dot(h, win_ref[:, 0:D_FF], preferred_element_type=F32)
    gate = jnp.dot(h, win_ref[:, D_FF:2 * D_FF], preferred_element_type=F32)
    t = lax.broadcasted_iota(jnp.int32, (sb, tl, D_FF), 1).reshape(sb * tl, D_FF)
    bc = lambda j: jnp.broadcast_to(carry_scr[:, j:j + 1, :], (sb, tl, D_FF)).reshape(sb * tl, D_FF)
    prev1 = jnp.where(t == 0, bc(1), pltpu.roll(u, 1, axis=0))
    prev2 = jnp.where(t == 0, bc(0), jnp.where(t == 1, bc(1), pltpu.roll(u, 2, axis=0)))
    u3 = u.reshape(sb, tl, D_FF)
    carry_scr[...] = u3[:, tl - 2:tl, :]
    uc = cb_ref[...] + prev2 * cw_ref[0:1, :]
    uc = uc + prev1 * cw_ref[1:2, :]
    uc = uc + u * cw_ref[2:3, :]
    act = jax.nn.gelu(uc) * gate
    y = x + jnp.dot(act.astype(BF16), wout_ref[...], preferred_element_type=F32)
    out_ref[...] = y.reshape(sb, tl, D_MODEL)

    @pl.when(ti == pl.num_programs(1) - 1)
    def _():
        cT_ref[...] = carry_scr[...]


def ffn(x, conv0, P, layer, sb, tl):
    nseq, L, _ = x.shape
    blk = pl.BlockSpec((sb, tl, D_MODEL), lambda s, t: (s, t, 0))
    cst = pl.BlockSpec((sb, CONV_W - 1, D_FF), lambda s, t: (s, 0, 0))
    return pl.pallas_call(
        _ffn_kernel,
        grid=(nseq // sb, L // tl),
        in_specs=[blk, cst, _layer((1, D_MODEL), layer), _layer((D_MODEL, 2 * D_FF), layer),
                  _layer((CONV_W, D_FF), layer), _layer((1, D_FF), layer), _layer((D_FF, D_MODEL), layer)],
        out_specs=[blk, cst],
        out_shape=[jax.ShapeDtypeStruct((nseq, L, D_MODEL), F32),
                   jax.ShapeDtypeStruct((nseq, CONV_W - 1, D_FF), F32)],
        scratch_shapes=[pltpu.VMEM((sb, CONV_W - 1, D_FF), F32)],
        compiler_params=_cparams(("parallel", "arbitrary")),
        name="ffn",
    )(x, conv0, P['norm2'], P['w_ffn_in'], P['conv_w'], P['conv_b'], P['w_ffn_out'])


def _norm_kernel(x_ref, g_ref, out_ref):
    out_ref[...] = _rms(x_ref[...], g_ref[...], NORM_EPS)


def final_norm(x, gain):
    n = x.shape[0]
    tm = min(ROW_TILE, n)
    row = pl.BlockSpec((tm, D_MODEL), lambda i: (i, 0))
    return pl.pallas_call(
        _norm_kernel,
        grid=(n // tm,),
        in_specs=[row, _full((1, D_MODEL))],
        out_specs=row,
        out_shape=jax.ShapeDtypeStruct((n, D_MODEL), F32),
        compiler_params=_cparams(("parallel",)),
        name="final_norm",
    )(x, gain)


def _prepare_params(W):
    P = {}
    w_in = W['w_in']
    P['w_a'] = w_in[:, :, 0:A_COLS].astype(BF16)
    P['w_b'] = w_in[:, :, A_COLS:A_COLS + B_COLS].astype(BF16)
    P['w_c'] = w_in[:, :, A_COLS + B_COLS:A_COLS + B_COLS + C_COLS].astype(BF16)
    P['w_gate'] = w_in[:, :, A_COLS + B_COLS + C_COLS:].astype(BF16)
    for name in ('w_pa', 'w_pb', 'w_pc', 'w_o', 'w_ffn_in', 'w_ffn_out'):
        P[name] = W[name].astype(BF16)
    for name in ('norm1', 'norm2', 'mu_a', 'w0_a', 'a0_a', 'k_k', 'k_a', 'lnx_w', 'lnx_b', 'conv_b',
                 'lam_q1', 'lam_k1', 'lam_q2', 'lam_k2', 'subln'):
        P[name] = W[name][:, None, :]
    P['r_k'] = W['r_k'].reshape(DEPTH, 1, W_A)
    P['w2_a'] = jnp.concatenate([W['w2_a'], jnp.zeros((DEPTH, R_A, W_A), F32)], axis=1)
    P['a2_a'] = jnp.concatenate([jnp.zeros((DEPTH, R_W, W_A), F32), W['a2_a']], axis=1)
    P['g2_a'] = W['g2_a']
    P['conv_w'] = W['conv_w']
    return P


def _layer_group(x, states, pos0, attn_fn, P, layer, ffn_blk):
    nseq, L, _ = x.shape
    wkv0, shift0, ret0, conv0 = states
    flat = lambda z: z.reshape(nseq * L, z.shape[-1])
    seq = lambda z: z.reshape(nseq, L, z.shape[-1])
    pa, pb, q, k, v = in_proj(flat(x), P['norm1'], P['w_a'], P['w_b'], P['w_c'], layer)
    out_a, wkv_t, shift_t = rwkv(seq(pa), shift0, wkv0, P, layer)
    out_b, ret_t = retention(seq(pb), ret0, pos0)
    lam_init = 0.8 - 0.6 * math.exp(-0.3 * layer)
    q, k, v = seq(q), seq(k), seq(v)
    out_c = attn_fn(q, k, v, lam_init)
    x = merge(flat(x), flat(out_a), flat(out_b), flat(out_c), P, layer)
    x, conv_t = ffn(seq(x), conv0, P, layer, *ffn_blk)
    return x, (k, v, wkv_t, shift_t, ret_t, conv_t)


def kernel(x_prompt, x_sample, cache_k, cache_v, page_table, state_wkv, state_shift, state_ret,
           state_conv, meta, norm1, w_in, mu_a, w0_a, w2_a, a0_a, a2_a, g2_a, k_k, k_a, r_k,
           lnx_w, lnx_b, lam_q1, lam_k1, lam_q2, lam_k2, subln, w_pa, w_pb, w_pc, w_o, norm2,
           w_ffn_in, conv_w, conv_b, w_ffn_out, norm_f):
    W = dict(norm1=norm1, w_in=w_in, mu_a=mu_a, w0_a=w0_a, w2_a=w2_a, a0_a=a0_a, a2_a=a2_a,
             g2_a=g2_a, k_k=k_k, k_a=k_a, r_k=r_k, lnx_w=lnx_w, lnx_b=lnx_b, lam_q1=lam_q1,
             lam_k1=lam_k1, lam_q2=lam_q2, lam_k2=lam_k2, subln=subln, w_pa=w_pa, w_pb=w_pb,
             w_pc=w_pc, w_o=w_o, norm2=norm2, w_ffn_in=w_ffn_in, conv_w=conv_w, conv_b=conv_b,
             w_ffn_out=w_ffn_out)
    P = _prepare_params(W)
    b, t_main, _ = x_prompt.shape
    bd, ds, _ = x_sample.shape
    past = page_table.shape[1] * PAGE_SIZE

    x_meta = jnp.broadcast_to(meta[None].astype(F32), (b, N_META, D_MODEL))
    x_main = x_prompt
    x_samp = x_sample
    zeros = lambda *s: jnp.zeros(s, F32)
    meta_states = (zeros(b, H_A, N_A, N_A), zeros(b, A_COLS), zeros(b, H_B, DK_B, DV_B),
                   zeros(b, CONV_W - 1, D_FF))
    outs_p = [[] for _ in range(6)]
    outs_s = [[] for _ in range(6)]
    for l in range(DEPTH):
        x_meta, st_meta = _layer_group(
            x_meta, meta_states, 0,
            lambda q, k, v, li: attn_prompt(q, k, v, None, P, l, li), P, l, (b, N_META))
        km, vm = st_meta[0], st_meta[1]
        x_main, st_main = _layer_group(
            x_main, st_meta[2:], N_META,
            lambda q, k, v, li: attn_prompt(q, k, v, (km, vm), P, l, li), P, l,
            (1, min(ROW_TILE, t_main)))
        x_samp, st_samp = _layer_group(
            x_samp, (state_wkv[l], state_shift[l], state_ret[l], state_conv[l]), past,
            lambda q, k, v, li: attn_sample(q, k, v, cache_k, cache_v, page_table, P, l, li), P, l,
            (bd, ds))
        heads = lambda z: z.reshape(z.shape[0], z.shape[1], H_C, DV_C)
        outs_p[0].append(jnp.concatenate([heads(km), heads(st_main[0])], axis=1))
        outs_p[1].append(jnp.concatenate([heads(vm), heads(st_main[1])], axis=1))
        for i in range(2, 6):
            outs_p[i].append(st_main[i])
        outs_s[0].append(heads(st_samp[0]))
        outs_s[1].append(heads(st_samp[1]))
        for i in range(2, 6):
            outs_s[i].append(st_samp[i])
    gain_f = norm_f[None, :]
    y_prompt = final_norm(x_main.reshape(b * t_main, D_MODEL), gain_f).reshape(b, t_main, D_MODEL)
    y_sample = final_norm(x_samp.reshape(bd * ds, D_MODEL), gain_f).reshape(bd, ds, D_MODEL)
    stack = lambda lst: jnp.stack(lst, 0)
    return (y_prompt, y_sample, *[stack(o) for o in outs_p], *[stack(o) for o in outs_s])
```

```python
import math

import jax
import jax.numpy as jnp
from jax import lax
from jax.experimental import pallas as pl
from jax.experimental.pallas import tpu as pltpu

F32 = jnp.float32
BF16 = jnp.bfloat16

D_MODEL = 1024
DEPTH = 4
N_META = 16
H_A, N_A = 8, 64
W_A = H_A * N_A
R_W, R_A, R_G = 64, 64, 128
H_B, DK_B, DV_B = 4, 64, 128
QK_B, W_B = H_B * DK_B, H_B * DV_B
ROPE_BASE = 10000.0
H_C, DH_C = 4, 64
DV_C = 2 * DH_C
W_C = H_C * DV_C
D_FF = 2816
CONV_W = 3
A_COLS = 3 * W_A + R_W + R_A + R_G
B_COLS = 2 * QK_B + 2 * W_B
C_COLS = 3 * W_C
NORM_EPS = 1e-6
GN_EPS_A = 64e-5
SUBLN_EPS = 1e-5
NEG_INF = -1e30
PAGE_SIZE = 128

LANES = 128
VMEM_LIMIT = 52 * 1024 * 1024
ROW_TILE = 512
RWKV_SEQS = 4
RWKV_TILE = 128
RWKV_GROUP = 4
RWKV_UNROLL = 8
RET_CHUNK = 128
RET_CHUNKS_PER_STEP = 4
ATTN_BLOCK = 1024
ATTN_ROWS = 256
PAGES_PER_STEP = 32

HIGHEST = lax.Precision.HIGHEST


def _cparams(sem):
    return pltpu.CompilerParams(dimension_semantics=sem, vmem_limit_bytes=VMEM_LIMIT)


def _rms(x, gain, eps):
    return x * lax.rsqrt(jnp.mean(x * x, axis=-1, keepdims=True) + eps) * gain


def _bdot(a, b):
    return jnp.dot(a.astype(BF16), b.astype(BF16), preferred_element_type=F32)


def _bdot_nt(a, b):
    return lax.dot_general(a.astype(BF16), b.astype(BF16), (((1,), (1,)), ((), ())),
                           preferred_element_type=F32)


def _bdot_tn(a, b):
    return lax.dot_general(a.astype(BF16), b.astype(BF16), (((0,), (0,)), ((), ())),
                           preferred_element_type=F32)


def _fdot(a, b):
    return jnp.dot(a, b, precision=HIGHEST, preferred_element_type=F32)


def _full(shape):
    return pl.BlockSpec(shape, lambda *_: (0,) * len(shape))


def _layer(shape, layer):
    return pl.BlockSpec((None,) + shape, lambda *_: (layer,) + (0,) * len(shape))


def _in_proj_kernel(x_ref, g_ref, wa_ref, wb_ref, wc_ref, pa_ref, pb_ref, q_ref, k_ref, v_ref):
    h = _rms(x_ref[...], g_ref[...], NORM_EPS).astype(BF16)
    pa_ref[...] = jnp.dot(h, wa_ref[...], preferred_element_type=F32)
    pb_ref[...] = jnp.dot(h, wb_ref[...], preferred_element_type=F32)
    q_ref[...] = jnp.dot(h, wc_ref[:, 0:W_C], preferred_element_type=F32)
    k_ref[...] = jnp.dot(h, wc_ref[:, W_C:2 * W_C], preferred_element_type=F32)
    v_ref[...] = jnp.dot(h, wc_ref[:, 2 * W_C:3 * W_C], preferred_element_type=F32)


def in_proj(x, gain, wa, wb, wc, layer):
    n = x.shape[0]
    tm = min(ROW_TILE, n)
    row = lambda c: pl.BlockSpec((tm, c), lambda i: (i, 0))
    return pl.pallas_call(
        _in_proj_kernel,
        grid=(n // tm,),
        in_specs=[row(D_MODEL), _layer((1, D_MODEL), layer), _layer((D_MODEL, A_COLS), layer),
                  _layer((D_MODEL, B_COLS), layer), _layer((D_MODEL, C_COLS), layer)],
        out_specs=[row(A_COLS), row(B_COLS), row(W_C), row(W_C), row(W_C)],
        out_shape=[jax.ShapeDtypeStruct((n, c), F32) for c in (A_COLS, B_COLS, W_C, W_C, W_C)],
        compiler_params=_cparams(("parallel",)),
        name="in_proj",
    )(x, gain, wa, wb, wc)


QUAD = 4 * N_A


def _block_ones(n, dtype):
    r = lax.broadcasted_iota(jnp.int32, (n, n), 0) // N_A
    c = lax.broadcasted_iota(jnp.int32, (n, n), 1) // N_A
    return (r == c).astype(dtype)


def _head_sum(x, ones):
    hi = x.astype(BF16)
    lo = (x - hi.astype(F32)).astype(BF16)
    return (jnp.dot(hi, ones, preferred_element_type=F32)
            + jnp.dot(lo, ones, preferred_element_type=F32))


def _rwkv_kernel(pa_ref, shift0_ref, s0_ref, mu_ref, w0_ref, w2_ref, a0_ref, a2_ref, g2_ref,
                 kk_ref, ka_ref, rk_ref, lnw_ref, lnb_ref,
                 out_ref, sT_ref, shiftT_ref,
                 s_scr, carry_scr, r_scr, y_scr, k_scr, v_scr, d_scr, al_scr, be_scr, g_scr, bonus_scr):
    nb, lt = pa_ref.shape[0], pa_ref.shape[1]
    nquad = W_A // QUAD
    ti = pl.program_id(1)

    @pl.when(ti == 0)
    def _():
        s_scr[...] = s0_ref[...].reshape(nb * nquad, N_A, QUAD)
        carry_scr[...] = shift0_ref[...]

    ones = _block_ones(W_A, BF16)
    row = lax.broadcasted_iota(jnp.int32, (lt, A_COLS), 0)
    for b in range(nb):
        x = pa_ref[b]
        prev = jnp.where(row == 0, carry_scr[b], pltpu.roll(x, 1, axis=0))
        carry_scr[b] = x[lt - 1:lt, :]
        xm = x + (prev - x) * mu_ref[...]
        r = xm[:, 0:W_A]
        k = xm[:, W_A:2 * W_A]
        v = xm[:, 2 * W_A:3 * W_A]
        lr = xm[:, 3 * W_A:3 * W_A + R_W + R_A]
        gd = xm[:, 3 * W_A + R_W + R_A:A_COLS]
        w = -jax.nn.softplus(-(w0_ref[...] + _fdot(jnp.tanh(lr), w2_ref[...]))) - 0.5
        a = jax.nn.sigmoid(a0_ref[...] + _fdot(lr, a2_ref[...]))
        kk = k * kk_ref[...]
        kk = kk * lax.rsqrt(jnp.maximum(_head_sum(kk * kk, ones), 1e-24))
        k = k * (1.0 + (a - 1.0) * ka_ref[...])
        g_scr[b] = _bdot(jax.nn.sigmoid(gd), g2_ref[...])
        bonus_scr[b] = _head_sum(r * k * rk_ref[...], ones) * v
        for q in range(nquad):
            i, cols = b * nquad + q, slice(q * QUAD, (q + 1) * QUAD)
            r_scr[i, 0:8, :] = jnp.zeros((8, QUAD), F32)
            r_scr[i, 8:lt + 8, :] = r[:, cols]
            k_scr[i] = k[:, cols]
            v_scr[i] = v[:, cols]
            d_scr[i] = jnp.exp(-jnp.exp(w))[:, cols]
            al_scr[i] = -kk[:, cols]
            be_scr[i] = (kk * a)[:, cols]

    ones_q = _block_ones(QUAD, BF16)
    lane = lax.broadcasted_iota(jnp.int32, (N_A, QUAD), 1)
    sub = lax.broadcasted_iota(jnp.int32, (N_A, QUAD), 0)
    diag = (lane % N_A) == sub

    def to_rows8(z):
        return jnp.sum(jnp.where(diag, z, 0.0).reshape(N_A // 8, 8, QUAD), axis=0)

    def put_y(i, rows, y8):
        y_scr[2 * i, rows, :] = y8[:, 0:LANES]
        y_scr[2 * i + 1, rows, :] = y8[:, LANES:QUAD]

    def step(t, carry):
        for g in range(0, nb * nquad, RWKV_GROUP):
            group = range(g, g + RWKV_GROUP)
            states, parts = [], []
            for i in group:
                s = s_scr[i]
                sb = s.astype(BF16)
                states.append(s)
                parts += [sb * al_scr[i, pl.ds(t, 1), :].astype(BF16),
                          sb * r_scr[i, pl.ds(t + 7, 1), :].astype(BF16),
                          jnp.where(diag, v_scr[i, pl.ds(t, 1), :], 0.0).astype(BF16)]
            o = jnp.dot(jnp.concatenate(parts, axis=0), ones_q, preferred_element_type=F32)
            for j, i in enumerate(group):
                sa, y_prev, vcol = (o[(3 * j + n) * N_A:(3 * j + n + 1) * N_A] for n in range(3))
                s_scr[i] = (states[j] * d_scr[i, pl.ds(t, 1), :] + sa * be_scr[i, pl.ds(t, 1), :]
                            + vcol * k_scr[i, pl.ds(t, 1), :])
                put_y(i, pl.ds(pl.multiple_of(t * 8, 8), 8), to_rows8(y_prev))
        return carry

    lax.fori_loop(0, lt, step, 0, unroll=min(RWKV_UNROLL, lt))

    for b in range(nb):
        for q in range(nquad):
            i = b * nquad + q
            o = jnp.dot((s_scr[i] * r_scr[i, lt + 7:lt + 8, :]).astype(BF16), ones_q,
                        preferred_element_type=F32)
            put_y(i, slice(lt * 8, (lt + 1) * 8), to_rows8(o))

        def half_y(n):
            tot = y_scr[n, pl.ds(8, lt, stride=8), :]
            for j in range(1, 8):
                tot = tot + y_scr[n, pl.ds(8 + j, lt, stride=8), :]
            return tot

        y = jnp.concatenate([half_y(2 * b * nquad + n) for n in range(2 * nquad)], axis=1)
        yc = y - _head_sum(y, ones) * (1.0 / N_A)
        yn = yc * lax.rsqrt(_head_sum(yc * yc, ones) * (1.0 / N_A) + GN_EPS_A)
        out_ref[b] = (yn * lnw_ref[...] + lnb_ref[...] + bonus_scr[b]) * g_scr[b]

    @pl.when(ti == pl.num_programs(1) - 1)
    def _():
        sT_ref[...] = s_scr[...].reshape(nb, nquad, N_A, QUAD)
        shiftT_ref[...] = carry_scr[...]


def _pack_wkv(s):
    n = s.shape[0]
    return s.reshape(n, H_A // 4, 4, N_A, N_A).transpose(0, 1, 3, 2, 4).reshape(n, H_A // 4, N_A, QUAD)


def _unpack_wkv(s):
    n = s.shape[0]
    return s.reshape(n, H_A // 4, N_A, 4, N_A).transpose(0, 1, 3, 2, 4).reshape(n, H_A, N_A, N_A)


def rwkv(pa, shift0, wkv0, P, layer):
    nseq, L, _ = pa.shape
    nb = RWKV_SEQS
    lt = min(RWKV_TILE, L)
    nquad = W_A // QUAD
    seq3 = lambda c: pl.BlockSpec((nb, lt, c), lambda s, t: (s, t, 0))
    vec = lambda name: _layer(P[name].shape[1:], layer)
    state = pl.BlockSpec((nb, nquad, N_A, QUAD), lambda s, t: (s, 0, 0, 0))
    slab = lambda rows: pltpu.VMEM((nb * nquad, rows, QUAD), F32)
    tile = lambda: pltpu.VMEM((nb, lt, W_A), F32)
    out, sT, shiftT = pl.pallas_call(
        _rwkv_kernel,
        grid=(nseq // nb, L // lt),
        in_specs=[seq3(A_COLS), pl.BlockSpec((nb, 1, A_COLS), lambda s, t: (s, 0, 0)), state,
                  vec('mu_a'), vec('w0_a'), vec('w2_a'), vec('a0_a'), vec('a2_a'), vec('g2_a'),
                  vec('k_k'), vec('k_a'), vec('r_k'), vec('lnx_w'), vec('lnx_b')],
        out_specs=[seq3(W_A), state, pl.BlockSpec((nb, 1, A_COLS), lambda s, t: (s, 0, 0))],
        out_shape=[jax.ShapeDtypeStruct((nseq, L, W_A), F32),
                   jax.ShapeDtypeStruct((nseq, nquad, N_A, QUAD), F32),
                   jax.ShapeDtypeStruct((nseq, 1, A_COLS), F32)],
        scratch_shapes=[pltpu.VMEM((nb * nquad, N_A, QUAD), F32), pltpu.VMEM((nb, 1, A_COLS), F32),
                        slab(lt + 8), pltpu.VMEM((2 * nb * nquad, (lt + 1) * 8, LANES), F32)]
                       + [slab(lt) for _ in range(5)] + [tile(), tile()],
        compiler_params=_cparams(("parallel", "arbitrary")),
        name="rwkv",
    )(pa, shift0[:, None, :], _pack_wkv(wkv0), P['mu_a'], P['w0_a'], P['w2_a'], P['a0_a'],
      P['a2_a'], P['g2_a'], P['k_k'], P['k_a'], P['r_k'], P['lnx_w'], P['lnx_b'])
    return out, _unpack_wkv(sT), shiftT[:, 0, :]


def _swap_halves(x):
    lane = lax.broadcasted_iota(jnp.int32, x.shape, 1)
    first = (lane % DK_B) < (DK_B // 2)
    return jnp.where(first, pltpu.roll(x, QK_B - DK_B // 2, axis=1), pltpu.roll(x, DK_B // 2, axis=1))


def _head_blocks():
    r = lax.broadcasted_iota(jnp.int32, (QK_B, W_B), 0) // DK_B
    c = lax.broadcasted_iota(jnp.int32, (QK_B, W_B), 1) // DV_B
    return r == c


def _retention_kernel(pb_ref, r0_ref, cos_ref, sin_ref, dmat_ref, rowdec_ref, keydec_ref, chdec_ref,
                      out_ref, rT_ref, r_scr):
    ci = pl.program_id(1)

    @pl.when(ci == 0)
    def _():
        r_scr[...] = jnp.zeros(r_scr.shape, F32)
        for h in range(H_B):
            r_scr[h * DK_B:(h + 1) * DK_B, h * DV_B:(h + 1) * DV_B] = r0_ref[h]

    lc = dmat_ref.shape[1]
    for c0 in range(0, pb_ref.shape[0], lc):
        rows = slice(c0, c0 + lc)
        x = pb_ref[rows, :]
        q = x[:, 0:QK_B]
        k = x[:, QK_B:2 * QK_B]
        v = x[:, 2 * QK_B:2 * QK_B + W_B]
        q = q * cos_ref[rows, :] + _swap_halves(q) * sin_ref[rows, :]
        k = (k * cos_ref[rows, :] + _swap_halves(k) * sin_ref[rows, :]) * (DK_B ** -0.5)
        state = r_scr[...]
        o_state = _bdot(q, state) * rowdec_ref[...]
        update = _bdot_tn(k * keydec_ref[...], v)
        r_scr[...] = state * chdec_ref[...] + jnp.where(_head_blocks(), update, 0.0)
        q_head = lax.broadcasted_iota(jnp.int32, q.shape, 1) // DK_B
        for h in range(H_B):
            cols = slice(h * DV_B, (h + 1) * DV_B)
            s = _bdot_nt(jnp.where(q_head == h, q, 0.0), k) * dmat_ref[h]
            o = _bdot(s, v[:, cols]) + o_state[:, cols]
            o = o * lax.rsqrt(jnp.mean(o * o, axis=-1, keepdims=True) + NORM_EPS)
            out_ref[rows, cols] = o * jax.nn.silu(x[:, 2 * QK_B + W_B + h * DV_B:2 * QK_B + W_B + (h + 1) * DV_B])

    @pl.when(ci == pl.num_programs(1) - 1)
    def _():
        for h in range(H_B):
            rT_ref[h] = r_scr[h * DK_B:(h + 1) * DK_B, h * DV_B:(h + 1) * DV_B]


def _retention_tables(L, lc, pos0):
    lg = jnp.log1p(-jnp.exp2(-5.0 - jnp.arange(H_B, dtype=F32)))
    idx = jnp.arange(lc, dtype=F32)
    diff = idx[:, None] - idx[None, :]
    dmat = jnp.where(diff >= 0, jnp.exp(jnp.maximum(diff, 0.0)[None] * lg[:, None, None]), 0.0)
    rowdec = jnp.repeat(jnp.exp((idx[:, None] + 1.0) * lg[None, :]), DV_B, axis=1)
    keydec = jnp.repeat(jnp.exp((lc - 1.0 - idx)[:, None] * lg[None, :]), DK_B, axis=1)
    chdec = jnp.broadcast_to(jnp.repeat(jnp.exp(lc * lg), DK_B)[:, None], (QK_B, W_B))
    half = DK_B // 2
    inv = ROPE_BASE ** (-jnp.arange(half, dtype=F32) / half)
    ang = (pos0 + jnp.arange(L)).astype(F32)[:, None] * inv[None, :]
    cos, sin = jnp.cos(ang), jnp.sin(ang)
    cos_t = jnp.tile(jnp.concatenate([cos, cos], axis=1), (1, H_B))
    sin_t = jnp.tile(jnp.concatenate([-sin, sin], axis=1), (1, H_B))
    return cos_t, sin_t, dmat, rowdec, keydec, chdec


def retention(pb, ret0, pos0):
    nseq, L, _ = pb.shape
    lc = min(RET_CHUNK, L)
    cos_t, sin_t, dmat, rowdec, keydec, chdec = _retention_tables(L, lc, pos0)
    state = pl.BlockSpec((None, H_B, DK_B, DV_B), lambda s, c: (s, 0, 0, 0))
    lb = min(RET_CHUNK * RET_CHUNKS_PER_STEP, L)
    tab = pl.BlockSpec((lb, QK_B), lambda s, c: (c, 0))
    out, rT = pl.pallas_call(
        _retention_kernel,
        grid=(nseq, L // lb),
        in_specs=[pl.BlockSpec((None, lb, B_COLS), lambda s, c: (s, c, 0)), state, tab, tab,
                  _full((H_B, lc, lc)), _full((lc, W_B)), _full((lc, QK_B)), _full((QK_B, W_B))],
        out_specs=[pl.BlockSpec((None, lb, W_B), lambda s, c: (s, c, 0)), state],
        out_shape=[jax.ShapeDtypeStruct((nseq, L, W_B), F32),
                   jax.ShapeDtypeStruct((nseq, H_B, DK_B, DV_B), F32)],
        scratch_shapes=[pltpu.VMEM((QK_B, W_B), F32)],
        compiler_params=_cparams(("parallel", "arbitrary")),
        name="retention",
    )(pb, ret0, cos_t, sin_t, dmat, rowdec, keydec, chdec)
    return out, rT


def _lam(lq1, lk1, lq2, lk2, lam_init):
    return (jnp.exp(jnp.sum(lq1 * lk1, axis=-1, keepdims=True))
            - jnp.exp(jnp.sum(lq2 * lk2, axis=-1, keepdims=True)) + lam_init)


def _two_map_queries(q):
    lane = lax.broadcasted_iota(jnp.int32, q.shape, 1)
    q = q * (DH_C ** -0.5)
    return jnp.concatenate([jnp.where(lane < DH_C, q, 0.0), jnp.where(lane < DH_C, 0.0, q)],
                           axis=0).astype(BF16)


def _softmax_update(s, pv, m_ref, l_ref, acc_ref, rows=slice(None)):
    m_old = m_ref[rows]
    m_new = jnp.maximum(m_old, jnp.max(s, axis=-1, keepdims=True))
    alpha = jnp.exp(m_old - m_new)
    p = jnp.exp(s - m_new[:, 0:1])
    l_ref[rows] = alpha * l_ref[rows] + jnp.sum(p, axis=-1, keepdims=True)
    acc_ref[rows] = alpha * acc_ref[rows] + pv(p)
    m_ref[rows] = m_new


def _attn_prompt_kernel(lam_init, has_prefix, qi_ref, kj_ref, q_ref, k_ref, v_ref, *rest):
    if has_prefix:
        kp_ref, vp_ref, *rest = rest
    lq1, lk1, lq2, lk2, subln_ref, out_ref, qb_scr, m_scr, acc_scr = rest
    step = pl.program_id(2)
    qi, kj = qi_ref[step], kj_ref[step]
    bq = q_ref.shape[0]

    def process(k, v, mask):
        k = k.astype(BF16)
        v = jnp.concatenate([v.astype(BF16), jnp.ones(v.shape, BF16)], axis=1)
        rc = min(ATTN_ROWS, bq)
        for r0 in range(0, 2 * bq, rc):
            rows = slice(r0, r0 + rc)
            s = _bdot_nt(qb_scr[rows], k)
            if mask:
                r = lax.broadcasted_iota(jnp.int32, s.shape, 0) + (r0 % bq)
                c = lax.broadcasted_iota(jnp.int32, s.shape, 1)
                s = jnp.where(c <= r, s, NEG_INF)
            m_old = m_scr[rows]
            m_new = jnp.maximum(m_old, jnp.max(s, axis=-1, keepdims=True))
            alpha = jnp.exp(m_old - m_new)
            p = jnp.exp(s - m_new[:, 0:1])
            acc_scr[rows] = jnp.concatenate([alpha, alpha], axis=1) * acc_scr[rows] + _bdot(p, v)
            m_scr[rows] = m_new

    @pl.when(kj == 0)
    def _():
        qb_scr[...] = _two_map_queries(q_ref[...])
        m_scr[...] = jnp.full(m_scr.shape, NEG_INF, F32)
        acc_scr[...] = jnp.zeros(acc_scr.shape, F32)
        if has_prefix:
            process(kp_ref[...], vp_ref[...], False)

    @pl.when(kj < qi)
    def _():
        process(k_ref[...], v_ref[...], False)

    @pl.when(kj == qi)
    def _():
        process(k_ref[...], v_ref[...], True)
        lam = _lam(lq1[...], lk1[...], lq2[...], lk2[...], lam_init)
        o = (acc_scr[0:bq, 0:DV_C] / acc_scr[0:bq, DV_C:2 * DV_C]
             - lam * (acc_scr[bq:2 * bq, 0:DV_C] / acc_scr[bq:2 * bq, DV_C:2 * DV_C]))
        out_ref[...] = _rms(o, subln_ref[...], SUBLN_EPS) * (1.0 - lam_init)


def attn_prompt(q, k, v, prefix, P, layer, lam_init):
    nseq, L, _ = q.shape
    blk = min(ATTN_BLOCK, L)
    nblk = L // blk
    pairs = [(i, j) for i in range(nblk) for j in range(i + 1)]
    qi_tab = jnp.array([p[0] for p in pairs], jnp.int32)
    kj_tab = jnp.array([p[1] for p in pairs], jnp.int32)
    qspec = pl.BlockSpec((None, blk, DV_C), lambda b, h, s, qi, kj: (b, qi[s], h))
    kspec = pl.BlockSpec((None, blk, DV_C), lambda b, h, s, qi, kj: (b, kj[s], h))
    args, specs = [q, k, v], [qspec, kspec, kspec]
    if prefix is not None:
        npre = prefix[0].shape[1]
        pspec = pl.BlockSpec((None, npre, DV_C), lambda b, h, s, qi, kj: (b, 0, h))
        args += list(prefix)
        specs += [pspec, pspec]
    lspec = lambda c: pl.BlockSpec((None, 1, c), lambda b, h, s, qi, kj: (layer, 0, 0))
    for name in ('lam_q1', 'lam_k1', 'lam_q2', 'lam_k2'):
        args.append(P[name])
        specs.append(lspec(DH_C))
    args.append(P['subln'])
    specs.append(lspec(DV_C))
    wide = lambda dt: pltpu.VMEM((2 * blk, DV_C), dt)
    grid_spec = pltpu.PrefetchScalarGridSpec(
        num_scalar_prefetch=2,
        grid=(nseq, H_C, len(pairs)),
        in_specs=specs,
        out_specs=qspec,
        scratch_shapes=[wide(BF16), wide(F32), pltpu.VMEM((2 * blk, 2 * DV_C), F32)],
    )
    return pl.pallas_call(
        lambda *refs: _attn_prompt_kernel(lam_init, prefix is not None, *refs),
        grid_spec=grid_spec,
        out_shape=jax.ShapeDtypeStruct((nseq, L, W_C), F32),
        compiler_params=_cparams(("parallel", "parallel", "arbitrary")),
        name="attn_prompt",
    )(qi_tab, kj_tab, *args)


def _attn_sample_kernel(lam_init, npp, pt_ref, q_ref, kn_ref, vn_ref, lq1, lk1, lq2, lk2, subln_ref,
                        *rest):
    k_refs, v_refs = rest[:npp], rest[npp:2 * npp]
    out_ref, qb_scr, m_scr, l_scr, acc_scr = rest[2 * npp:]
    j = pl.program_id(1)
    nq = q_ref.shape[0]
    hr = 2 * nq
    head = lambda h: slice(h * DV_C, (h + 1) * DV_C)
    rows = lambda h: slice(h * hr, (h + 1) * hr)

    @pl.when(j == 0)
    def _():
        for h in range(H_C):
            qb_scr[rows(h), :] = _two_map_queries(q_ref[:, head(h)])
        m_scr[...] = jnp.full(m_scr.shape, NEG_INF, F32)
        l_scr[...] = jnp.zeros(l_scr.shape, F32)
        acc_scr[...] = jnp.zeros(acc_scr.shape, F32)

    of_head = lambda ref, h: ref[pl.ds(h, PAGE_SIZE, stride=H_C), :]
    s = jnp.concatenate(
        [jnp.concatenate([_bdot_nt(qb_scr[rows(h), :], of_head(k_refs[i], h)) for i in range(npp)], axis=1)
         for h in range(H_C)], axis=0)

    def pv_pages(p):
        outs = []
        for h in range(H_C):
            tot = None
            for i in range(npp):
                t = _bdot(p[rows(h), i * PAGE_SIZE:(i + 1) * PAGE_SIZE], of_head(v_refs[i], h))
                tot = t if tot is None else tot + t
            outs.append(tot)
        return jnp.concatenate(outs, axis=0)

    _softmax_update(s, pv_pages, m_scr, l_scr, acc_scr)

    @pl.when(j == pl.num_programs(1) - 1)
    def _():
        sn = jnp.concatenate([_bdot_nt(qb_scr[rows(h), :], kn_ref[:, head(h)]) for h in range(H_C)], axis=0)
        r = lax.broadcasted_iota(jnp.int32, sn.shape, 0)
        c = lax.broadcasted_iota(jnp.int32, sn.shape, 1)
        sn = jnp.where(c <= r % nq, sn, NEG_INF)
        _softmax_update(
            sn, lambda p: jnp.concatenate([_bdot(p[rows(h), :], vn_ref[:, head(h)]) for h in range(H_C)], axis=0),
            m_scr, l_scr, acc_scr)
        lam = _lam(lq1[...], lk1[...], lq2[...], lk2[...], lam_init)
        o_all = acc_scr[...] / l_scr[...]
        for h in range(H_C):
            o = o_all[h * hr:h * hr + nq] - lam * o_all[h * hr + nq:(h + 1) * hr]
            out_ref[:, head(h)] = _rms(o, subln_ref[...], SUBLN_EPS) * (1.0 - lam_init)


def attn_sample(q, k, v, cache_k, cache_v, page_table, P, layer, lam_init):
    nseq, nq, _ = q.shape
    n_pages = page_table.shape[1]
    npp = PAGES_PER_STEP
    seq = pl.BlockSpec((None, nq, W_C), lambda b, j, pt: (b, 0, 0))

    n_pool = cache_k.shape[1]
    cache_k = cache_k.reshape(DEPTH, n_pool, PAGE_SIZE * H_C, DV_C)
    cache_v = cache_v.reshape(DEPTH, n_pool, PAGE_SIZE * H_C, DV_C)

    def page(i):
        return pl.BlockSpec((None, None, PAGE_SIZE * H_C, DV_C),
                            lambda b, j, pt: (layer, pt[b * n_pages + j * npp + i], 0, 0))

    lspec = lambda c: pl.BlockSpec((None, 1, c), lambda b, j, pt: (layer, 0, 0))
    rows = H_C * 2 * nq
    grid_spec = pltpu.PrefetchScalarGridSpec(
        num_scalar_prefetch=1,
        grid=(nseq, n_pages // npp),
        in_specs=[seq, seq, seq, lspec(DH_C), lspec(DH_C), lspec(DH_C), lspec(DH_C), lspec(DV_C)]
                 + [page(i) for i in range(npp)] + [page(i) for i in range(npp)],
        out_specs=seq,
        scratch_shapes=[pltpu.VMEM((rows, DV_C), BF16), pltpu.VMEM((rows, DV_C), F32),
                        pltpu.VMEM((rows, DV_C), F32), pltpu.VMEM((rows, DV_C), F32)],
    )
    return pl.pallas_call(
        lambda *refs: _attn_sample_kernel(lam_init, npp, *refs),
        grid_spec=grid_spec,
        out_shape=jax.ShapeDtypeStruct((nseq, nq, W_C), F32),
        compiler_params=_cparams(("parallel", "arbitrary")),
        name="attn_sample",
    )(page_table.reshape(-1), q, k, v, P['lam_q1'], P['lam_k1'], P['lam_q2'], P['lam_k2'], P['subln'],
      *([cache_k] * npp), *([cache_v] * npp))


def _merge_kernel(x_ref, oa_ref, ob_ref, oc_ref, g_ref, wg_ref, wpa_ref, wpb_ref, wpc_ref, wo_ref,
                  out_ref):
    x = x_ref[...]
    h = _rms(x, g_ref[...], NORM_EPS).astype(BF16)
    merged = None
    for i, (o_ref, w_ref) in enumerate(((oa_ref, wpa_ref), (ob_ref, wpb_ref), (oc_ref, wpc_ref))):
        gate = jax.nn.sigmoid(jnp.dot(h, wg_ref[:, i * D_MODEL:(i + 1) * D_MODEL],
                                      preferred_element_type=F32))
        term = gate * jnp.dot(o_ref[...].astype(BF16), w_ref[...], preferred_element_type=F32)
        merged = term if merged is None else merged + term
    out_ref[...] = x + jnp.dot(merged.astype(BF16), wo_ref[...], preferred_element_type=F32)


def merge(x, oa, ob, oc, P, layer):
    n = x.shape[0]
    tm = min(ROW_TILE, n)
    row = lambda c: pl.BlockSpec((tm, c), lambda i: (i, 0))
    return pl.pallas_call(
        _merge_kernel,
        grid=(n // tm,),
        in_specs=[row(D_MODEL), row(W_A), row(W_B), row(W_C), _layer((1, D_MODEL), layer),
                  _layer((D_MODEL, 3 * D_MODEL), layer), _layer((W_A, D_MODEL), layer),
                  _layer((W_B, D_MODEL), layer), _layer((W_C, D_MODEL), layer),
                  _layer((D_MODEL, D_MODEL), layer)],
        out_specs=row(D_MODEL),
        out_shape=jax.ShapeDtypeStruct((n, D_MODEL), F32),
        compiler_params=_cparams(("parallel",)),
        name="merge",
    )(x, oa, ob, oc, P['norm1'], P['w_gate'], P['w_pa'], P['w_pb'], P['w_pc'], P['w_o'])


def _ffn_kernel(x_ref, c0_ref, g_ref, win_ref, cw_ref, cb_ref, wout_ref, out_ref, cT_ref, carry_scr):
    sb, tl = x_ref.shape[0], x_ref.shape[1]
    ti = pl.program_id(1)

    @pl.when(ti == 0)
    def _():
        carry_scr[...] = c0_ref[...]

    x = x_ref[...].reshape(sb * tl, D_MODEL)
    h = _rms(x, g_ref[...], NORM_EPS).astype(BF16)
    u = jnp.dot(h, win_ref[:, 0:D_FF], preferred_element_type=F32)
    gate = jnp.dot(h, win_ref[:, D_FF:2 * D_FF], preferred_element_type=F32)
    t = lax.broadcasted_iota(jnp.int32, (sb, tl, D_FF), 1).reshape(sb * tl, D_FF)
    bc = lambda j: jnp.broadcast_to(carry_scr[:, j:j + 1, :], (sb, tl, D_FF)).reshape(sb * tl, D_FF)
    prev1 = jnp.where(t == 0, bc(1), pltpu.roll(u, 1, axis=0))
    prev2 = jnp.where(t == 0, bc(0), jnp.where(t == 1, bc(1), pltpu.roll(u, 2, axis=0)))
    u3 = u.reshape(sb, tl, D_FF)
    carry_scr[...] = u3[:, tl - 2:tl, :]
    uc = cb_ref[...] + prev2 * cw_ref[0:1, :]
    uc = uc + prev1 * cw_ref[1:2, :]
    uc = uc + u * cw_ref[2:3, :]
    act = jax.nn.gelu(uc) * gate
    y = x + jnp.dot(act.astype(BF16), wout_ref[...], preferred_element_type=F32)
    out_ref[...] = y.reshape(sb, tl, D_MODEL)

    @pl.when(ti == pl.num_programs(1) - 1)
    def _():
        cT_ref[...] = carry_scr[...]


def ffn(x, conv0, P, layer, sb, tl):
    nseq, L, _ = x.shape
    blk = pl.BlockSpec((sb, tl, D_MODEL), lambda s, t: (s, t, 0))
    cst = pl.BlockSpec((sb, CONV_W - 1, D_FF), lambda s, t: (s, 0, 0))
    return pl.pallas_call(
        _ffn_kernel,
        grid=(nseq // sb, L // tl),
        in_specs=[blk, cst, _layer((1, D_MODEL), layer), _layer((D_MODEL, 2 * D_FF), layer),
                  _layer((CONV_W, D_FF), layer), _layer((1, D_FF), layer), _layer((D_FF, D_MODEL), layer)],
        out_specs=[blk, cst],
        out_shape=[jax.ShapeDtypeStruct((nseq, L, D_MODEL), F32),
                   jax.ShapeDtypeStruct((nseq, CONV_W - 1, D_FF), F32)],
        scratch_shapes=[pltpu.VMEM((sb, CONV_W - 1, D_FF), F32)],
        compiler_params=_cparams(("parallel", "arbitrary")),
        name="ffn",
    )(x, conv0, P['norm2'], P['w_ffn_in'], P['conv_w'], P['conv_b'], P['w_ffn_out'])


def _norm_kernel(x_ref, g_ref, out_ref):
    out_ref[...] = _rms(x_ref[...], g_ref[...], NORM_EPS)


def final_norm(x, gain):
    n = x.shape[0]
    tm = min(ROW_TILE, n)
    row = pl.BlockSpec((tm, D_MODEL), lambda i: (i, 0))
    return pl.pallas_call(
        _norm_kernel,
        grid=(n // tm,),
        in_specs=[row, _full((1, D_MODEL))],
        out_specs=row,
        out_shape=jax.ShapeDtypeStruct((n, D_MODEL), F32),
        compiler_params=_cparams(("parallel",)),
        name="final_norm",
    )(x, gain)


def _prepare_params(W):
    P = {}
    w_in = W['w_in']
    P['w_a'] = w_in[:, :, 0:A_COLS].astype(BF16)
    P['w_b'] = w_in[:, :, A_COLS:A_COLS + B_COLS].astype(BF16)
    P['w_c'] = w_in[:, :, A_COLS + B_COLS:A_COLS + B_COLS + C_COLS].astype(BF16)
    P['w_gate'] = w_in[:, :, A_COLS + B_COLS + C_COLS:].astype(BF16)
    for name in ('w_pa', 'w_pb', 'w_pc', 'w_o', 'w_ffn_in', 'w_ffn_out'):
        P[name] = W[name].astype(BF16)
    for name in ('norm1', 'norm2', 'mu_a', 'w0_a', 'a0_a', 'k_k', 'k_a', 'lnx_w', 'lnx_b', 'conv_b',
                 'lam_q1', 'lam_k1', 'lam_q2', 'lam_k2', 'subln'):
        P[name] = W[name][:, None, :]
    P['r_k'] = W['r_k'].reshape(DEPTH, 1, W_A)
    P['w2_a'] = jnp.concatenate([W['w2_a'], jnp.zeros((DEPTH, R_A, W_A), F32)], axis=1)
    P['a2_a'] = jnp.concatenate([jnp.zeros((DEPTH, R_W, W_A), F32), W['a2_a']], axis=1)
    P['g2_a'] = W['g2_a']
    P['conv_w'] = W['conv_w']
    return P


def _layer_group(x, states, pos0, attn_fn, P, layer, ffn_blk):
    nseq, L, _ = x.shape
    wkv0, shift0, ret0, conv0 = states
    flat = lambda z: z.reshape(nseq * L, z.shape[-1])
    seq = lambda z: z.reshape(nseq, L, z.shape[-1])
    pa, pb, q, k, v = in_proj(flat(x), P['norm1'], P['w_a'], P['w_b'], P['w_c'], layer)
    out_a, wkv_t, shift_t = rwkv(seq(pa), shift0, wkv0, P, layer)
    out_b, ret_t = retention(seq(pb), ret0, pos0)
    lam_init = 0.8 - 0.6 * math.exp(-0.3 * layer)
    q, k, v = seq(q), seq(k), seq(v)
    out_c = attn_fn(q, k, v, lam_init)
    x = merge(flat(x), flat(out_a), flat(out_b), flat(out_c), P, layer)
    x, conv_t = ffn(seq(x), conv0, P, layer, *ffn_blk)
    return x, (k, v, wkv_t, shift_t, ret_t, conv_t)


def kernel(x_prompt, x_sample, cache_k, cache_v, page_table, state_wkv, state_shift, state_ret,
           state_conv, meta, norm1, w_in, mu_a, w0_a, w2_a, a0_a, a2_a, g2_a, k_k, k_a, r_k,
           lnx_w, lnx_b, lam_q1, lam_k1, lam_q2, lam_k2, subln, w_pa, w_pb, w_pc, w_o, norm2,
           w_ffn_in, conv_w, conv_b, w_ffn_out, norm_f):
    W = dict(norm1=norm1, w_in=w_in, mu_a=mu_a, w0_a=w0_a, w2_a=w2_a, a0_a=a0_a, a2_a=a2_a,
             g2_a=g2_a, k_k=k_k, k_a=k_a, r_k=r_k, lnx_w=lnx_w, lnx_b=lnx_b, lam_q1=lam_q1,
             lam_k1=lam_k1, lam_q2=lam_q2, lam_k2=lam_k2, subln=subln, w_pa=w_pa, w_pb=w_pb,
             w_pc=w_pc, w_o=w_o, norm2=norm2, w_ffn_in=w_ffn_in, conv_w=conv_w, conv_b=conv_b,
             w_ffn_out=w_ffn_out)
    P = _prepare_params(W)
    b, t_main, _ = x_prompt.shape
    bd, ds, _ = x_sample.shape
    past = page_table.shape[1] * PAGE_SIZE

    x_meta = jnp.broadcast_to(meta[None].astype(F32), (b, N_META, D_MODEL))
    x_main = x_prompt
    x_samp = x_sample
    zeros = lambda *s: jnp.zeros(s, F32)
    meta_states = (zeros(b, H_A, N_A, N_A), zeros(b, A_COLS), zeros(b, H_B, DK_B, DV_B),
                   zeros(b, CONV_W - 1, D_FF))
    outs_p = [[] for _ in range(6)]
    outs_s = [[] for _ in range(6)]
    for l in range(DEPTH):
        x_meta, st_meta = _layer_group(
            x_meta, meta_states, 0,
            lambda q, k, v, li: attn_prompt(q, k, v, None, P, l, li), P, l, (b, N_META))
        km, vm = st_meta[0], st_meta[1]
        x_main, st_main = _layer_group(
            x_main, st_meta[2:], N_META,
            lambda q, k, v, li: attn_prompt(q, k, v, (km, vm), P, l, li), P, l,
            (1, min(ROW_TILE, t_main)))
        x_samp, st_samp = _layer_group(
            x_samp, (state_wkv[l], state_shift[l], state_ret[l], state_conv[l]), past,
            lambda q, k, v, li: attn_sample(q, k, v, cache_k, cache_v, page_table, P, l, li), P, l,
            (bd, ds))
        heads = lambda z: z.reshape(z.shape[0], z.shape[1], H_C, DV_C)
        outs_p[0].append(jnp.concatenate([heads(km), heads(st_main[0])], axis=1))
        outs_p[1].append(jnp.concatenate([heads(vm), heads(st_main[1])], axis=1))
        for i in range(2, 6):
            outs_p[i].append(st_main[i])
        outs_s[0].append(heads(st_samp[0]))
        outs_s[1].append(heads(st_samp[1]))
        for i in range(2, 6):
            outs_s[i].append(st_samp[i])
    gain_f = norm_f[None, :]
    y_prompt = final_norm(x_main.reshape(b * t_main, D_MODEL), gain_f).reshape(b, t_main, D_MODEL)
    y_sample = final_norm(x_samp.reshape(bd * ds, D_MODEL), gain_f).reshape(bd, ds, D_MODEL)
    stack = lambda lst: jnp.stack(lst, 0)
    return (y_prompt, y_sample, *[stack(o) for o in outs_p], *[stack(o) for o in outs_s])
```
